```python
import math
import jax, jax.numpy as jnp
from jax import lax
import numpy as np

D_MODEL = 1024
BATCH = 4
SEQ = 4096
DEPTH = 2
DEC_BATCH = 2
DEC_SEQ = 8192
PAST_LEN = 128

N_MIXERS = 2
N_LRU_LAYERS = (DEPTH + 1) // 2
N_MLA_LAYERS = DEPTH // 2

LRU_WIDTH = D_MODEL
LRU_BLOCKS = 4
LRU_BLOCK_W = LRU_WIDTH // LRU_BLOCKS
CONV_WIDTH = 4
CONV_LEFT = 2
LRU_C = 8.0

N_HEADS = 8
QK_NOPE = 128
QK_ROPE = 64
QK_HEAD = QK_NOPE + QK_ROPE
V_HEAD = 128
Q_LORA = 384
KV_LORA = 256
ROPE_THETA = 10000.0
Q_BLOCK = 128

N_GROUPS = 4
EXPERTS_PER_GROUP = 8
N_EXPERTS = N_GROUPS * EXPERTS_PER_GROUP
TOP_K = 2
D_EXPERT = 512

EPS = 1e-6

kernel_name = "hybrid_bidir_rglru_mla_hmoe"


def rms_norm(x, g):
    xf = x.astype(jnp.float32)
    y = xf * lax.rsqrt(jnp.mean(xf * xf, axis=-1, keepdims=True) + EPS)
    return (y * g.astype(jnp.float32)).astype(x.dtype)


def centred_dwconv(x, w, b):
    C = x.shape[-1]
    y = lax.conv_general_dilated(
        x, w[:, None, :], window_strides=(1,),
        padding=[(CONV_LEFT, CONV_WIDTH - 1 - CONV_LEFT)],
        dimension_numbers=("NWC", "WIO", "NWC"), feature_group_count=C)
    return y + b


def block_diag(x, w, b):
    xb = x.reshape(*x.shape[:-1], LRU_BLOCKS, LRU_BLOCK_W)
    y = jnp.einsum("bsnc,ncd->bsnd", xb, w)
    return y.reshape(x.shape) + b


def _lin_combine(c1, c2):
    a1, b1 = c1
    a2, b2 = c2
    return a1 * a2, a2 * b1 + b2


def rg_lru(x, w_a, b_a, w_i, b_i, lam):
    r = jax.nn.sigmoid(block_diag(x, w_a, b_a).astype(jnp.float32))
    i = jax.nn.sigmoid(block_diag(x, w_i, b_i).astype(jnp.float32))
    log_a = -LRU_C * r * jax.nn.softplus(-lam.astype(jnp.float32))
    a = jnp.exp(log_a)
    u = jnp.sqrt(-jnp.expm1(2.0 * log_a)) * (i * x.astype(jnp.float32))
    _, h = lax.associative_scan(_lin_combine, (a, u), axis=1)
    return h


def lru_mixer(x, w_in, conv_w, conv_b, w_a, b_a, w_i, b_i, lam, w_out):
    gate, xr = jnp.split(x @ w_in, 2, axis=-1)
    xc = centred_dwconv(xr, conv_w, conv_b)
    h_fwd = rg_lru(xc, w_a[0], b_a[0], w_i[0], b_i[0], lam[0])
    h_bwd = jnp.flip(rg_lru(jnp.flip(xc, 1), w_a[1], b_a[1], w_i[1], b_i[1], lam[1]), 1)
    y = (h_fwd + h_bwd).astype(x.dtype) * jax.nn.gelu(gate, approximate=True)
    return y @ w_out


def rope_tables(S):
    inv = 1.0 / (ROPE_THETA ** (jnp.arange(0, QK_ROPE, 2, dtype=jnp.float32) / QK_ROPE))
    ang = jnp.arange(S, dtype=jnp.float32)[:, None] * inv[None, :]
    return jnp.cos(ang), jnp.sin(ang)


def apply_rope(x, cos, sin):
    xf = x.astype(jnp.float32)
    x1, x2 = jnp.split(xf, 2, axis=-1)
    c = cos[None, :, None, :]
    s = sin[None, :, None, :]
    return jnp.concatenate([x1 * c - x2 * s, x1 * s + x2 * c], axis=-1).astype(x.dtype)


def block_attention(q, k, v):
    B, S, H, Dq = q.shape
    nb = S // Q_BLOCK
    qb = q.reshape(B, nb, Q_BLOCK, H, Dq).transpose(1, 0, 2, 3, 4)
    scale = 1.0 / math.sqrt(QK_HEAD)

    def one(qblk):
        s = jnp.einsum("bqhd,bkhd->bhqk", qblk, k, preferred_element_type=jnp.float32) * scale
        p = jax.nn.softmax(s, axis=-1)
        return jnp.einsum("bhqk,bkhd->bqhd", p.astype(v.dtype), v)

    o = lax.map(one, qb)
    return o.transpose(1, 0, 2, 3, 4).reshape(B, S, H, V_HEAD)


def mla_mixer(x, w_in, q_lat_norm, w_uq, kv_lat_norm, w_ukv, q_head_norm, k_head_norm, w_o):
    B, S, _ = x.shape
    c_q, c_kv, k_pe = jnp.split(x @ w_in, [Q_LORA, Q_LORA + KV_LORA], axis=-1)
    q = (rms_norm(c_q, q_lat_norm) @ w_uq).reshape(B, S, N_HEADS, QK_HEAD)
    kv = (rms_norm(c_kv, kv_lat_norm) @ w_ukv).reshape(B, S, N_HEADS, QK_NOPE + V_HEAD)
    k_nope, v = jnp.split(kv, [QK_NOPE], axis=-1)
    k = jnp.concatenate(
        [k_nope, jnp.broadcast_to(k_pe[:, :, None, :], (B, S, N_HEADS, QK_ROPE))], axis=-1)
    q = rms_norm(q, q_head_norm)
    k = rms_norm(k, k_head_norm)
    cos, sin = rope_tables(S)
    q = jnp.concatenate([q[..., :QK_NOPE], apply_rope(q[..., QK_NOPE:], cos, sin)], axis=-1)
    k = jnp.concatenate([k[..., :QK_NOPE], apply_rope(k[..., QK_NOPE:], cos, sin)], axis=-1)
    o = block_attention(q, k, v)
    return o.reshape(B, S, N_HEADS * V_HEAD) @ w_o


def hier_moe(x, w_group, b_group, w_router, b_router, w_gate, w_up, w_down):
    B, S, D = x.shape
    xt = x.reshape(-1, D)
    g_prob = jax.nn.softmax((xt @ w_group).astype(jnp.float32) + b_group.astype(jnp.float32), axis=-1)
    g_top_p, g_top = lax.top_k(g_prob, 1)
    e_logits = ((xt @ w_router).astype(jnp.float32) + b_router.astype(jnp.float32)
                ).reshape(-1, N_GROUPS, EXPERTS_PER_GROUP)
    e_sel = jnp.take_along_axis(e_logits, g_top[:, :, None], axis=1)[:, 0]
    top_v, top_i = lax.top_k(e_sel, TOP_K)
    w = jax.nn.softmax(top_v, axis=-1) * g_top_p
    eidx = g_top * EXPERTS_PER_GROUP + top_i
    combine = jnp.einsum("tk,tke->te", w, jax.nn.one_hot(eidx, N_EXPERTS, dtype=jnp.float32))
    y = jnp.zeros(xt.shape, jnp.float32)
    for e in range(N_EXPERTS):
        h = jax.nn.silu(xt @ w_gate[e]) * (xt @ w_up[e])
        y = y + combine[:, e:e + 1] * (h @ w_down[e]).astype(jnp.float32)
    return y.astype(x.dtype).reshape(B, S, D)


def trunk(x, mix_norm, ffn_norm, lru, mla, moe):
    for layer in range(DEPTH):
        j = layer // N_MIXERS
        h = rms_norm(x, mix_norm[layer])
        if layer % N_MIXERS == 0:
            x = x + lru_mixer(h, *[p[j] for p in lru])
        else:
            x = x + mla_mixer(h, *[p[j] for p in mla])
        x = x + hier_moe(rms_norm(x, ffn_norm[layer]), *[p[layer] for p in moe])
    return x


def setup_inputs(seed: int = 0) -> dict:
    key = jax.random.key(seed)
    ks = iter(jax.random.split(key, 64))
    f32 = jnp.float32

    def nrm(shape, fan_in):
        return jax.random.normal(next(ks), shape, f32) * (fan_in ** -0.5)

    def gain(shape):
        return 1.0 + 0.05 * jax.random.normal(next(ks), shape, f32)

    def bias(shape, s=0.02):
        return s * jax.random.normal(next(ks), shape, f32)

    NL, NM = N_LRU_LAYERS, N_MLA_LAYERS
    u = jax.random.uniform(next(ks), (NL, 2, LRU_WIDTH), f32, 0.9, 0.999)
    s = u ** (1.0 / LRU_C)
    lam = jnp.log(s) - jnp.log1p(-s)

    return {
        "x_prompt": jax.random.normal(next(ks), (BATCH, SEQ, D_MODEL), f32),
        "x_sample": jax.random.normal(next(ks), (DEC_BATCH, DEC_SEQ, D_MODEL), f32),
        "mix_norm": gain((DEPTH, D_MODEL)),
        "ffn_norm": gain((DEPTH, D_MODEL)),
        "lru_w_in": nrm((NL, D_MODEL, 2 * LRU_WIDTH), D_MODEL),
        "lru_conv_w": nrm((NL, CONV_WIDTH, LRU_WIDTH), CONV_WIDTH),
        "lru_conv_b": bias((NL, LRU_WIDTH)),
        "lru_w_a": nrm((NL, 2, LRU_BLOCKS, LRU_BLOCK_W, LRU_BLOCK_W), LRU_BLOCK_W),
        "lru_b_a": bias((NL, 2, LRU_WIDTH)),
        "lru_w_i": nrm((NL, 2, LRU_BLOCKS, LRU_BLOCK_W, LRU_BLOCK_W), LRU_BLOCK_W),
        "lru_b_i": bias((NL, 2, LRU_WIDTH)),
        "lru_lambda": lam,
        "lru_w_out": nrm((NL, LRU_WIDTH, D_MODEL), LRU_WIDTH),
        "mla_w_in": nrm((NM, D_MODEL, Q_LORA + KV_LORA + QK_ROPE), D_MODEL),
        "mla_q_lat_norm": gain((NM, Q_LORA)),
        "mla_w_uq": nrm((NM, Q_LORA, N_HEADS * QK_HEAD), Q_LORA),
        "mla_kv_lat_norm": gain((NM, KV_LORA)),
        "mla_w_ukv": nrm((NM, KV_LORA, N_HEADS * (QK_NOPE + V_HEAD)), KV_LORA),
        "mla_q_head_norm": gain((NM, QK_HEAD)),
        "mla_k_head_norm": gain((NM, QK_HEAD)),
        "mla_w_o": nrm((NM, N_HEADS * V_HEAD, D_MODEL), N_HEADS * V_HEAD),
        "moe_w_group": nrm((DEPTH, D_MODEL, N_GROUPS), D_MODEL),
        "moe_b_group": bias((DEPTH, N_GROUPS), 0.01),
        "moe_w_router": nrm((DEPTH, D_MODEL, N_EXPERTS), D_MODEL),
        "moe_b_router": bias((DEPTH, N_EXPERTS), 0.01),
        "moe_w_gate": nrm((DEPTH, N_EXPERTS, D_MODEL, D_EXPERT), D_MODEL),
        "moe_w_up": nrm((DEPTH, N_EXPERTS, D_MODEL, D_EXPERT), D_MODEL),
        "moe_w_down": nrm((DEPTH, N_EXPERTS, D_EXPERT, D_MODEL), D_EXPERT),
    }


def reference(x_prompt, x_sample, mix_norm, ffn_norm,
              lru_w_in, lru_conv_w, lru_conv_b, lru_w_a, lru_b_a, lru_w_i, lru_b_i, lru_lambda, lru_w_out,
              mla_w_in, mla_q_lat_norm, mla_w_uq, mla_kv_lat_norm, mla_w_ukv,
              mla_q_head_norm, mla_k_head_norm, mla_w_o,
              moe_w_group, moe_b_group, moe_w_router, moe_b_router, moe_w_gate, moe_w_up, moe_w_down):
    lru = (lru_w_in, lru_conv_w, lru_conv_b, lru_w_a, lru_b_a, lru_w_i, lru_b_i, lru_lambda, lru_w_out)
    mla = (mla_w_in, mla_q_lat_norm, mla_w_uq, mla_kv_lat_norm, mla_w_ukv,
           mla_q_head_norm, mla_k_head_norm, mla_w_o)
    moe = (moe_w_group, moe_b_group, moe_w_router, moe_b_router, moe_w_gate, moe_w_up, moe_w_down)
    y_prompt = trunk(x_prompt, mix_norm, ffn_norm, lru, mla, moe)
    y_sample = trunk(x_sample, mix_norm, ffn_norm, lru, mla, moe)
    return (y_prompt, y_sample)
```

```python
import functools
import math

import jax
import jax.numpy as jnp
from jax import lax
from jax.experimental import pallas as pl
from jax.experimental.pallas import tpu as pltpu

F32 = jnp.float32
BF16 = jnp.bfloat16
I32 = jnp.int32

D_MODEL = 1024
LRU_WIDTH = 1024
LRU_BLOCKS = 4
LRU_BLOCK_W = LRU_WIDTH // LRU_BLOCKS
CONV_WIDTH = 4
CONV_LEFT = 2
LRU_C = 8.0
N_HEADS = 8
QK_NOPE = 128
QK_ROPE = 64
QK_HEAD = QK_NOPE + QK_ROPE
V_HEAD = 128
Q_LORA = 384
KV_LORA = 256
ROPE_THETA = 10000.0
N_GROUPS = 4
EXPERTS_PER_GROUP = 8
N_EXPERTS = N_GROUPS * EXPERTS_PER_GROUP
D_EXPERT = 512
EPS = 1e-6

LANES = 128
SUBLANES = 8
HEAD_PAD = 2 * LANES
VMEM_LIMIT = 56 * 1024 * 1024
NEG_BIG = -3.0e38


def _cparams(sem):
    return pltpu.CompilerParams(dimension_semantics=sem, vmem_limit_bytes=VMEM_LIMIT)


def _rms(x, g):
    ms = jnp.mean(x * x, axis=-1, keepdims=True)
    return x * lax.rsqrt(ms + EPS) * g


def _dot(a, b):
    return jnp.dot(a, b, preferred_element_type=F32)


def _lru_in_body(xp_ref, xc_ref, xn_ref, g_ref, wg_ref, wr_ref, cw_ref, cb_ref,
                 gg_ref, xco_ref, xext_ref, rext_ref, *, ts):
    i = pl.program_id(1)
    last = pl.num_programs(1) - 1
    xext_ref[0:SUBLANES, :] = xp_ref[0]
    xext_ref[SUBLANES:SUBLANES + ts, :] = xc_ref[0]
    xext_ref[SUBLANES + ts:2 * SUBLANES + ts, :] = xn_ref[0]
    h = _rms(xext_ref[...], g_ref[...])
    gate = _dot(h[SUBLANES:SUBLANES + ts].astype(BF16), wg_ref[...])
    gg_ref[0] = jax.nn.gelu(gate, approximate=True)
    xr = _dot(h.astype(BF16), wr_ref[...])
    row = lax.broadcasted_iota(I32, (ts + 2 * SUBLANES, 1), 0)
    keep = jnp.logical_and(jnp.logical_or(row >= SUBLANES, i > 0),
                           jnp.logical_or(row < SUBLANES + ts, i < last))
    rext_ref[...] = jnp.where(keep, xr, 0.0)
    acc = jnp.broadcast_to(cb_ref[...], (ts, LRU_WIDTH))
    for k in range(CONV_WIDTH):
        acc = acc + rext_ref[pl.ds(SUBLANES - CONV_LEFT + k, ts), :] * cw_ref[k:k + 1, :]
    xco_ref[0] = acc


def _lru_in(x, g, w_gate, w_rec, conv_w, conv_b, *, ts):
    B, S, D = x.shape
    nc = S // ts
    r8 = ts // SUBLANES
    n8 = S // SUBLANES
    full = lambda shape: pl.BlockSpec(shape, lambda b, i: (0,) * len(shape))
    return pl.pallas_call(
        functools.partial(_lru_in_body, ts=ts),
        grid=(B, nc),
        in_specs=[
            pl.BlockSpec((1, SUBLANES, D), lambda b, i: (b, jnp.maximum(i * r8 - 1, 0), 0)),
            pl.BlockSpec((1, ts, D), lambda b, i: (b, i, 0)),
            pl.BlockSpec((1, SUBLANES, D), lambda b, i: (b, jnp.minimum((i + 1) * r8, n8 - 1), 0)),
            full((1, D)), full((D, LRU_WIDTH)), full((D, LRU_WIDTH)),
            full((CONV_WIDTH, LRU_WIDTH)), full((1, LRU_WIDTH)),
        ],
        out_specs=[pl.BlockSpec((1, ts, LRU_WIDTH), lambda b, i: (b, i, 0)),
                   pl.BlockSpec((1, ts, LRU_WIDTH), lambda b, i: (b, i, 0))],
        out_shape=[jax.ShapeDtypeStruct((B, S, LRU_WIDTH), F32),
                   jax.ShapeDtypeStruct((B, S, LRU_WIDTH), F32)],
        scratch_shapes=[pltpu.VMEM((ts + 2 * SUBLANES, D), F32),
                        pltpu.VMEM((ts + 2 * SUBLANES, LRU_WIDTH), F32)],
        compiler_params=_cparams(("parallel", "parallel")),
        name="lru_in",
    )(x, x, x, g, w_gate, w_rec, conv_w, conv_b)


def _lru_scan_body(*refs, ts, nb, reverse, final):
    if final:
        xc_ref, wa_ref, ba_ref, wi_ref, bi_ref, lam_ref, hf_ref, gg_ref, out_ref = refs[:9]
        scr = refs[9:]
    else:
        xc_ref, wa_ref, ba_ref, wi_ref, bi_ref, lam_ref, out_ref = refs[:7]
        scr = refs[7:]
    a_bufs, u_bufs, h_bufs = scr[0:nb], scr[nb:2 * nb], scr[2 * nb:3 * nb]
    carry_ref = scr[3 * nb]
    pitch = ts + SUBLANES
    groups = LRU_WIDTH // LANES

    @pl.when(pl.program_id(1) == 0)
    def _():
        carry_ref[...] = jnp.zeros_like(carry_ref)

    neg_c_sp = -LRU_C * jax.nn.softplus(-lam_ref[...])
    for bb in range(nb):
        xc = xc_ref[bb]
        xb = xc.astype(BF16)

        def block_diag(w_ref, b_ref):
            parts = [_dot(xb[:, n * LRU_BLOCK_W:(n + 1) * LRU_BLOCK_W], w_ref[n])
                     for n in range(LRU_BLOCKS)]
            return jnp.concatenate(parts, axis=-1) + b_ref[...]

        r = jax.nn.sigmoid(block_diag(wa_ref, ba_ref))
        ig = jax.nn.sigmoid(block_diag(wi_ref, bi_ref))
        log_a = neg_c_sp * r
        a = jnp.exp(log_a)
        u = jnp.sqrt(1.0 - a * a) * (ig * xc)
        for g in range(groups):
            a_bufs[bb][pl.ds(g * pitch, ts), :] = a[:, g * LANES:(g + 1) * LANES]
            u_bufs[bb][pl.ds(g * pitch, ts), :] = u[:, g * LANES:(g + 1) * LANES]

    def step(t, hs):
        tt = (ts - 1 - t) if reverse else t
        new = []
        for bb in range(nb):
            a_t = a_bufs[bb][pl.ds(tt, groups, stride=pitch), :]
            u_t = u_bufs[bb][pl.ds(tt, groups, stride=pitch), :]
            h = a_t * hs[bb] + u_t
            h_bufs[bb][pl.ds(tt, groups, stride=pitch), :] = h
            new.append(h)
        return tuple(new)

    hs = lax.fori_loop(0, ts, step, tuple(carry_ref[bb] for bb in range(nb)), unroll=8)
    for bb in range(nb):
        carry_ref[bb] = hs[bb]
        h_full = jnp.concatenate([h_bufs[bb][pl.ds(g * pitch, ts), :] for g in range(groups)], axis=-1)
        if final:
            out_ref[bb] = ((hf_ref[bb] + h_full) * gg_ref[bb]).astype(BF16)
        else:
            out_ref[bb] = h_full


def _lru_scan(xc, w_a, b_a, w_i, b_i, lam, hf=None, gg=None, *, ts, nb, reverse):
    B, S, W = xc.shape
    nc = S // ts
    final = hf is not None
    pitch = ts + SUBLANES
    cidx = (lambda b, i: (b, nc - 1 - i, 0)) if reverse else (lambda b, i: (b, i, 0))
    full = lambda shape: pl.BlockSpec(shape, lambda b, i: (0,) * len(shape))
    blk = pl.BlockSpec((nb, ts, W), cidx)
    in_specs = [blk, full((LRU_BLOCKS, LRU_BLOCK_W, LRU_BLOCK_W)), full((1, W)),
                full((LRU_BLOCKS, LRU_BLOCK_W, LRU_BLOCK_W)), full((1, W)), full((1, W))]
    args = [xc, w_a, b_a, w_i, b_i, lam]
    if final:
        in_specs += [blk, blk]
        args += [hf, gg]
    slab = pltpu.VMEM((W // LANES * pitch, LANES), F32)
    return pl.pallas_call(
        functools.partial(_lru_scan_body, ts=ts, nb=nb, reverse=reverse, final=final),
        grid=(B // nb, nc),
        in_specs=in_specs,
        out_specs=pl.BlockSpec((nb, ts, W), cidx),
        out_shape=jax.ShapeDtypeStruct((B, S, W), BF16 if final else F32),
        scratch_shapes=[slab] * (3 * nb) + [pltpu.VMEM((nb, W // LANES, LANES), F32)],
        compiler_params=_cparams(("parallel", "arbitrary")),
        name="lru_scan_bwd" if reverse else "lru_scan_fwd",
    )(*args)


def _post_mixer_body(x_ref, y_ref, w_ref, g_ref, wrh_ref, wrl_ref, br_ref,
                     x1_ref, xn_ref, mi_ref, mw_ref):
    x1 = x_ref[...] + _dot(y_ref[...], w_ref[...])
    x1_ref[...] = x1
    xn = _rms(x1, g_ref[...])
    xn_ref[...] = xn
    tm = xn.shape[0]
    xh = xn.astype(BF16)
    xl = (xn - xh.astype(F32)).astype(BF16)
    lg = _dot(xh, wrh_ref[...]) + (_dot(xl, wrh_ref[...]) + _dot(xh, wrl_ref[...])) + br_ref[...]
    lane = lax.broadcasted_iota(I32, (tm, LANES), 1)
    is_group = jnp.logical_and(lane >= N_EXPERTS, lane < N_EXPERTS + N_GROUPS)
    glog = jnp.where(is_group, lg, NEG_BIG)
    gmax = jnp.max(glog, axis=-1, keepdims=True)
    gsum = jnp.sum(jnp.where(is_group, jnp.exp(glog - gmax), 0.0), axis=-1, keepdims=True)
    g_top_p = 1.0 / gsum
    g_top = jnp.min(jnp.where(glog == gmax, lane, 4 * LANES), axis=-1, keepdims=True) - N_EXPERTS
    in_group = jnp.logical_and(lane < N_EXPERTS, (lane // EXPERTS_PER_GROUP) == g_top)
    sel = jnp.where(in_group, lg, NEG_BIG)
    v1 = jnp.max(sel, axis=-1, keepdims=True)
    i1 = jnp.min(jnp.where(sel == v1, lane, 4 * LANES), axis=-1, keepdims=True)
    sel2 = jnp.where(lane == i1, NEG_BIG, sel)
    v2 = jnp.max(sel2, axis=-1, keepdims=True)
    i2 = jnp.min(jnp.where(sel2 == v2, lane, 4 * LANES), axis=-1, keepdims=True)
    e2 = jnp.exp(v2 - v1)
    den = 1.0 + e2
    w1 = (1.0 / den) * g_top_p
    w2 = (e2 / den) * g_top_p
    mi_ref[...] = jnp.where(lane == 0, i1, jnp.where(lane == 1, i2, 0))
    mw_ref[...] = jnp.where(lane == 0, w1, jnp.where(lane == 1, w2, 0.0))


def _post_mixer(x, y, w, g, wr_hi, wr_lo, br, *, tm):
    T, D = x.shape
    K = y.shape[1]
    full = lambda shape: pl.BlockSpec(shape, lambda i: (0,) * len(shape))
    row = lambda n: pl.BlockSpec((tm, n), lambda i: (i, 0))
    return pl.pallas_call(
        _post_mixer_body,
        grid=(T // tm,),
        in_specs=[row(D), row(K), full((K, D)), full((1, D)),
                  full((D, LANES)), full((D, LANES)), full((1, LANES))],
        out_specs=[row(D), row(D), row(LANES), row(LANES)],
        out_shape=[jax.ShapeDtypeStruct((T, D), F32), jax.ShapeDtypeStruct((T, D), F32),
                   jax.ShapeDtypeStruct((T, LANES), I32), jax.ShapeDtypeStruct((T, LANES), F32)],
        compiler_params=_cparams(("parallel",)),
        name="post_mixer",
    )(x, y, w, g, wr_hi, wr_lo, br)


def _moe_body(te_ref, nu_ref, nv_ref, src_ref, dst_ref,
              xn_hbm, wg_ref, wu_ref, wd_ref, out_hbm,
              xbuf, obuf, wgb, wub, wdb, gsem, ssem, *, tm):
    j = pl.program_id(0)
    nu = nu_ref[0]
    slot = lax.rem(j, 2)

    def gather_start(tile, s):
        base = tile * tm

        def body(r, c):
            tok = src_ref[base + r]
            pltpu.make_async_copy(xn_hbm.at[pl.ds(tok, 1), :], xbuf.at[s, pl.ds(r, 1), :],
                                  gsem.at[s]).start()
            return c

        lax.fori_loop(0, tm, body, 0, unroll=8)

    def gather_wait(s):
        pltpu.make_async_copy(xn_hbm.at[pl.ds(0, tm), :], xbuf.at[s], gsem.at[s]).wait()

    def scatter_start(tile, s):
        base = tile * tm

        def body(r, c):
            row = dst_ref[base + r]
            pltpu.make_async_copy(obuf.at[s, pl.ds(r, 1), :], out_hbm.at[pl.ds(row, 1), :],
                                  ssem.at[s]).start()
            return c

        lax.fori_loop(0, nv_ref[tile], body, 0)

    def scatter_wait(tile, s):
        n = nv_ref[tile]
        n8 = pl.multiple_of((n // SUBLANES) * SUBLANES, SUBLANES)

        @pl.when(n8 > 0)
        def _():
            pltpu.make_async_copy(obuf.at[s, pl.ds(0, n8), :], out_hbm.at[pl.ds(0, n8), :], ssem.at[s]).wait()

        def body(r, c):
            pltpu.make_async_copy(obuf.at[s, pl.ds(0, 1), :], out_hbm.at[pl.ds(0, 1), :], ssem.at[s]).wait()
            return c

        lax.fori_loop(0, n - n8, body, 0)

    @pl.when(j < nu)
    def _():
        @pl.when(j == 0)
        def _():
            gather_start(0, 0)

        gather_wait(slot)

        @pl.when(j + 1 < nu)
        def _():
            gather_start(j + 1, 1 - slot)

        changed = jnp.logical_or(j == 0, te_ref[j] != te_ref[jnp.maximum(j - 1, 0)])

        @pl.when(changed)
        def _():
            wgb[...] = wg_ref[0].astype(BF16)
            wub[...] = wu_ref[0].astype(BF16)
            wdb[...] = wd_ref[0].astype(BF16)

        @pl.when(j >= 2)
        def _():
            scatter_wait(j - 2, slot)

        x = xbuf[slot].astype(BF16)
        h = (jax.nn.silu(_dot(x, wgb[...])) * _dot(x, wub[...])).astype(BF16)
        obuf[slot] = _dot(h, wdb[...])
        scatter_start(j, slot)

        @pl.when(j == nu - 1)
        def _():
            scatter_wait(j, slot)

            @pl.when(j >= 1)
            def _():
                scatter_wait(j - 1, 1 - slot)


def _moe_experts(xn, tile_e, n_used, n_valid, src, dst, w_gate, w_up, w_down, *, tm):
    T, D = xn.shape
    nt = src.shape[0] // tm
    wspec = lambda shape: pl.BlockSpec((1,) + shape, lambda j, te, nu, nv, s, d: (te[j], 0, 0))
    grid_spec = pltpu.PrefetchScalarGridSpec(
        num_scalar_prefetch=5,
        grid=(nt,),
        in_specs=[pl.BlockSpec(memory_space=pl.ANY),
                  wspec((D, D_EXPERT)), wspec((D, D_EXPERT)), wspec((D_EXPERT, D))],
        out_specs=pl.BlockSpec(memory_space=pl.ANY),
        scratch_shapes=[pltpu.VMEM((2, tm, D), F32), pltpu.VMEM((2, tm, D), F32),
                        pltpu.VMEM((D, D_EXPERT), BF16), pltpu.VMEM((D, D_EXPERT), BF16),
                        pltpu.VMEM((D_EXPERT, D), BF16),
                        pltpu.SemaphoreType.DMA((2,)), pltpu.SemaphoreType.DMA((2,))],
    )
    return pl.pallas_call(
        functools.partial(_moe_body, tm=tm),
        grid_spec=grid_spec,
        out_shape=jax.ShapeDtypeStruct((2 * T, D), F32),
        compiler_params=_cparams(("arbitrary",)),
        name="moe_experts",
    )(tile_e, n_used, n_valid, src, dst, xn, w_gate, w_up, w_down)


def _dispatch_tables(mi, *, tm):
    T = mi.shape[0]
    n = 2 * T
    P = n + N_EXPERTS * tm
    e = mi[:, :2].reshape(-1)
    order = jnp.argsort(e, stable=True).astype(I32)
    counts = jnp.sum((e[:, None] == jnp.arange(N_EXPERTS, dtype=I32)[None, :]).astype(I32), axis=0)
    padded = ((counts + tm - 1) // tm) * tm
    pend = jnp.cumsum(padded)
    pstart = pend - padded
    cstart = jnp.cumsum(counts) - counts
    p = jnp.arange(P, dtype=I32)
    ep = jnp.minimum(jnp.searchsorted(pend, p, side="right").astype(I32), N_EXPERTS - 1)
    local = p - pstart[ep]
    valid = jnp.logical_and(local < counts[ep], p < pend[N_EXPERTS - 1])
    slot = order[jnp.clip(cstart[ep] + local, 0, n - 1)]
    dst = jnp.where(valid, slot, 0).astype(I32)
    src = jnp.where(valid, slot // 2, 0).astype(I32)
    n_valid = jnp.sum(valid.reshape(P // tm, tm).astype(I32), axis=1)
    n_used = (pend[N_EXPERTS - 1] // tm).astype(I32)
    tile_e = ep[::tm]
    tile_e = jnp.where(jnp.arange(P // tm) < n_used, tile_e, tile_e[jnp.maximum(n_used - 1, 0)])
    return tile_e.astype(I32), n_used.reshape(1), n_valid, src, dst


def _combine_body(x1_ref, o2_ref, mw_ref, y_ref):
    w = mw_ref[...]
    o2 = o2_ref[...]
    y_ref[...] = x1_ref[...] + (w[:, 0:1] * o2[:, :D_MODEL] + w[:, 1:2] * o2[:, D_MODEL:])


def _moe_combine(x1, out2, mw, *, tm):
    T, D = x1.shape
    o2 = out2.reshape(out2.shape[0] // 2, 2 * D)
    return pl.pallas_call(
        _combine_body,
        grid=(T // tm,),
        in_specs=[pl.BlockSpec((tm, D), lambda i: (i, 0)),
                  pl.BlockSpec((tm, 2 * D), lambda i: (i, 0)),
                  pl.BlockSpec((tm, LANES), lambda i: (i, 0))],
        out_specs=pl.BlockSpec((tm, D), lambda i: (i, 0)),
        out_shape=jax.ShapeDtypeStruct((T, D), F32),
        compiler_params=_cparams(("parallel",)),
        name="moe_combine",
    )(x1, o2, mw)


def _mla_proj_body(x_ref, g_ref, win_ref, gq_ref, wuq_ref, gkv_ref, wuk_ref, wuv_ref,
                   gqh_ref, gkh_ref, cos_ref, sa_ref, sb_ref, q_ref, k_ref, v_ref):
    h = _rms(x_ref[0], g_ref[...]).astype(BF16)
    c = _dot(h, win_ref[...])
    cq = _rms(c[:, :Q_LORA], gq_ref[...]).astype(BF16)
    ckv = _rms(c[:, Q_LORA:Q_LORA + KV_LORA], gkv_ref[...]).astype(BF16)
    kpe = c[:, Q_LORA + KV_LORA:]
    q = _dot(cq, wuq_ref[...])
    kn = _dot(ckv, wuk_ref[...])
    v_ref[0] = _dot(ckv, wuv_ref[...]).astype(BF16)
    cos_t, sin_a, sin_b = cos_ref[...], sa_ref[...], sb_ref[...]
    gqh, gkh = gqh_ref[...], gkh_ref[...]

    def rope(r):
        return r * cos_t + pltpu.roll(r, LANES - QK_ROPE // 2, 1) * sin_a + pltpu.roll(r, QK_ROPE // 2, 1) * sin_b

    kpe_ss = jnp.sum(kpe * kpe, axis=-1, keepdims=True)
    for hh in range(N_HEADS):
        lo = hh * HEAD_PAD
        qn = q[:, lo:lo + LANES]
        qr = q[:, lo + LANES:lo + HEAD_PAD]
        ms = (jnp.sum(qn * qn, axis=-1, keepdims=True) + jnp.sum(qr * qr, axis=-1, keepdims=True)) / QK_HEAD
        sc = lax.rsqrt(ms + EPS)
        q_ref[0, :, lo:lo + LANES] = (qn * sc * gqh[:, :LANES]).astype(BF16)
        q_ref[0, :, lo + LANES:lo + HEAD_PAD] = rope(qr * sc * gqh[:, LANES:]).astype(BF16)
        kh = kn[:, hh * QK_NOPE:(hh + 1) * QK_NOPE]
        ms = (jnp.sum(kh * kh, axis=-1, keepdims=True) + kpe_ss) / QK_HEAD
        sc = lax.rsqrt(ms + EPS)
        k_ref[0, :, lo:lo + LANES] = (kh * sc * gkh[:, :LANES]).astype(BF16)
        k_ref[0, :, lo + LANES:lo + HEAD_PAD] = rope(kpe * sc * gkh[:, LANES:]).astype(BF16)


def _mla_proj(x, g, w_in, gq, w_uq, gkv, w_uk, w_uv, gqh, gkh, cos_t, sin_a, sin_b, *, tm):
    B, S, D = x.shape
    full = lambda shape: pl.BlockSpec(shape, lambda b, i: (0,) * len(shape))
    pos = pl.BlockSpec((tm, LANES), lambda b, i: (i, 0))
    HP = N_HEADS * HEAD_PAD
    HV = N_HEADS * V_HEAD
    return pl.pallas_call(
        _mla_proj_body,
        grid=(B, S // tm),
        in_specs=[pl.BlockSpec((1, tm, D), lambda b, i: (b, i, 0)), full((1, D)),
                  full(w_in.shape), full((1, Q_LORA)), full(w_uq.shape), full((1, KV_LORA)),
                  full(w_uk.shape), full(w_uv.shape), full((1, HEAD_PAD)), full((1, HEAD_PAD)),
                  pos, pos, pos],
        out_specs=[pl.BlockSpec((1, tm, HP), lambda b, i: (b, i, 0)),
                   pl.BlockSpec((1, tm, HP), lambda b, i: (b, i, 0)),
                   pl.BlockSpec((1, tm, HV), lambda b, i: (b, i, 0))],
        out_shape=[jax.ShapeDtypeStruct((B, S, HP), BF16), jax.ShapeDtypeStruct((B, S, HP), BF16),
                   jax.ShapeDtypeStruct((B, S, HV), BF16)],
        compiler_params=_cparams(("parallel", "parallel")),
        name="mla_proj",
    )(x, g, w_in, gq, w_uq, gkv, w_uk, w_uv, gqh, gkh, cos_t, sin_a, sin_b)


def _attn_body(q_ref, k_ref, v_ref, o_ref, *, tk, nkv):
    q = q_ref[0]
    tq = q.shape[0]
    scale = 1.0 / math.sqrt(QK_HEAD)

    def body(c, carry):
        m, l, acc = carry
        start = pl.multiple_of(c * tk, tk)
        kc = k_ref[0, pl.ds(start, tk), :]
        vc = v_ref[0, pl.ds(start, tk), :]
        s = lax.dot_general(q, kc, (((1,), (1,)), ((), ())), preferred_element_type=F32) * scale
        m_new = jnp.maximum(m, jnp.max(s, axis=-1, keepdims=True))
        alpha = jnp.exp(m - m_new)
        p = jnp.exp(s - m_new)
        l = alpha * l + jnp.sum(p, axis=-1, keepdims=True)
        acc = alpha * acc + _dot(p.astype(BF16), vc)
        return m_new, l, acc

    init = (jnp.full((tq, 1), -jnp.inf, F32), jnp.zeros((tq, 1), F32), jnp.zeros((tq, V_HEAD), F32))
    _, l, acc = lax.fori_loop(0, nkv, body, init)
    o_ref[0] = (acc / l).astype(BF16)


def _attention(q, k, v, *, tq, tk):
    B, S, _ = q.shape
    return pl.pallas_call(
        functools.partial(_attn_body, tk=tk, nkv=S // tk),
        grid=(B, N_HEADS, S // tq),
        in_specs=[pl.BlockSpec((1, tq, HEAD_PAD), lambda b, h, i: (b, i, h)),
                  pl.BlockSpec((1, S, HEAD_PAD), lambda b, h, i: (b, 0, h)),
                  pl.BlockSpec((1, S, V_HEAD), lambda b, h, i: (b, 0, h))],
        out_specs=pl.BlockSpec((1, tq, V_HEAD), lambda b, h, i: (b, i, h)),
        out_shape=jax.ShapeDtypeStruct((B, S, N_HEADS * V_HEAD), BF16),
        compiler_params=_cparams(("parallel", "parallel", "arbitrary")),
        name="mla_attention",
    )(q, k, v)


def _rope_tables(S):
    inv = 1.0 / (ROPE_THETA ** (jnp.arange(0, QK_ROPE, 2, dtype=F32) / QK_ROPE))
    ang = jnp.arange(S, dtype=F32)[:, None] * inv[None, :]
    cos, sin = jnp.cos(ang), jnp.sin(ang)
    z = jnp.zeros_like(cos)
    cos_t = jnp.concatenate([cos, cos, z, z], axis=-1)
    sin_a = jnp.concatenate([-sin, z, z, z], axis=-1)
    sin_b = jnp.concatenate([z, sin, z, z], axis=-1)
    return cos_t, sin_a, sin_b


def _head_pad_cols(w_nope, w_rope):
    K = w_nope.shape[0]
    z = jnp.zeros((K, N_HEADS, HEAD_PAD - QK_HEAD), w_nope.dtype)
    return jnp.concatenate([w_nope, w_rope, z], axis=-1).reshape(K, N_HEADS * HEAD_PAD)


def _router_params(w_group, b_group, w_router, b_router):
    pad = LANES - N_EXPERTS - N_GROUPS
    w = jnp.concatenate([w_router, w_group, jnp.zeros((D_MODEL, pad), F32)], axis=-1)
    b = jnp.concatenate([b_router, b_group, jnp.zeros((pad,), F32)]).reshape(1, LANES)
    w_hi = w.astype(BF16)
    w_lo = (w - w_hi.astype(F32)).astype(BF16)
    return w_hi, w_lo, b


def _tile(n, pref):
    t = min(n, pref)
    assert n % t == 0, (n, t)
    return t


def _moe(x, y, w_proj, g, router, w_gate, w_up, w_down, *, final_shape):
    T = x.shape[0]
    tm = _tile(T, 512)
    x1, xn, mi, mw = _post_mixer(x, y, w_proj, g, *router, tm=tm)
    te = _tile(T, 256)
    tile_e, n_used, n_valid, src, dst = _dispatch_tables(mi, tm=te)
    out2 = _moe_experts(xn, tile_e, n_used, n_valid, src, dst, w_gate, w_up, w_down, tm=te)
    return _moe_combine(x1, out2, mw, tm=tm).reshape(final_shape)


def _trunk(x, p):
    B, S, D = x.shape
    T = B * S
    ts = _tile(S, 256)
    nb = 2 if B % 2 == 0 else 1
    gg, xc = _lru_in(x, p["mix_g"][0], p["lru_w_gate"], p["lru_w_rec"], p["conv_w"], p["conv_b"], ts=ts)
    hf = _lru_scan(xc, p["w_a"][0], p["b_a"][0], p["w_i"][0], p["b_i"][0], p["lam"][0],
                   ts=ts, nb=nb, reverse=False)
    y = _lru_scan(xc, p["w_a"][1], p["b_a"][1], p["w_i"][1], p["b_i"][1], p["lam"][1], hf, gg,
                  ts=ts, nb=nb, reverse=True)
    x = _moe(x.reshape(T, D), y.reshape(T, LRU_WIDTH), p["lru_w_out"], p["ffn_g"][0], p["router"][0],
             p["moe_w_gate"][0], p["moe_w_up"][0], p["moe_w_down"][0], final_shape=(B, S, D))
    tm = _tile(S, 256)
    cos_t, sin_a, sin_b = _rope_tables(S)
    q, k, v = _mla_proj(x, p["mix_g"][1], p["mla_w_in"], p["gq"], p["w_uq"], p["gkv"], p["w_uk"], p["w_uv"],
                        p["gqh"], p["gkh"], cos_t, sin_a, sin_b, tm=tm)
    o = _attention(q, k, v, tq=_tile(S, 256), tk=_tile(S, 512))
    x = _moe(x.reshape(T, D), o.reshape(T, N_HEADS * V_HEAD), p["mla_w_o"], p["ffn_g"][1], p["router"][1],
             p["moe_w_gate"][1], p["moe_w_up"][1], p["moe_w_down"][1], final_shape=(B, S, D))
    return x


def _prepare(mix_norm, ffn_norm, lru_w_in, lru_conv_w, lru_conv_b, lru_w_a, lru_b_a, lru_w_i, lru_b_i,
             lru_lambda, lru_w_out, mla_w_in, mla_q_lat_norm, mla_w_uq, mla_kv_lat_norm, mla_w_ukv,
             mla_q_head_norm, mla_k_head_norm, mla_w_o, moe_w_group, moe_b_group, moe_w_router,
             moe_b_router, moe_w_gate, moe_w_up, moe_w_down):
    W = LRU_WIDTH
    w_uq = mla_w_uq[0].reshape(Q_LORA, N_HEADS, QK_HEAD)
    w_ukv = mla_w_ukv[0].reshape(KV_LORA, N_HEADS, QK_NOPE + V_HEAD)
    head_gain = lambda g: jnp.concatenate([g, jnp.zeros((HEAD_PAD - QK_HEAD,), F32)]).reshape(1, HEAD_PAD)
    w_in = jnp.concatenate([mla_w_in[0], jnp.zeros((D_MODEL, LANES - QK_ROPE), F32)], axis=-1)
    return {
        "mix_g": mix_norm.reshape(-1, 1, D_MODEL),
        "ffn_g": ffn_norm.reshape(-1, 1, D_MODEL),
        "lru_w_gate": lru_w_in[0, :, :W].astype(BF16),
        "lru_w_rec": lru_w_in[0, :, W:].astype(BF16),
        "conv_w": lru_conv_w[0],
        "conv_b": lru_conv_b[0].reshape(1, W),
        "w_a": lru_w_a[0].astype(BF16),
        "b_a": lru_b_a[0].reshape(2, 1, W),
        "w_i": lru_w_i[0].astype(BF16),
        "b_i": lru_b_i[0].reshape(2, 1, W),
        "lam": lru_lambda[0].reshape(2, 1, W),
        "lru_w_out": lru_w_out[0].astype(BF16),
        "mla_w_in": w_in.astype(BF16),
        "gq": mla_q_lat_norm[0].reshape(1, Q_LORA),
        "w_uq": _head_pad_cols(w_uq[:, :, :QK_NOPE], w_uq[:, :, QK_NOPE:]).astype(BF16),
        "gkv": mla_kv_lat_norm[0].reshape(1, KV_LORA),
        "w_uk": w_ukv[:, :, :QK_NOPE].reshape(KV_LORA, N_HEADS * QK_NOPE).astype(BF16),
        "w_uv": w_ukv[:, :, QK_NOPE:].reshape(KV_LORA, N_HEADS * V_HEAD).astype(BF16),
        "gqh": head_gain(mla_q_head_norm[0]),
        "gkh": head_gain(mla_k_head_norm[0]),
        "mla_w_o": mla_w_o[0].astype(BF16),
        "router": [_router_params(moe_w_group[l], moe_b_group[l], moe_w_router[l], moe_b_router[l])
                   for l in range(2)],
        "moe_w_gate": moe_w_gate, "moe_w_up": moe_w_up, "moe_w_down": moe_w_down,
    }


def kernel(x_prompt, x_sample, mix_norm, ffn_norm, lru_w_in, lru_conv_w, lru_conv_b, lru_w_a, lru_b_a, lru_w_i, lru_b_i, lru_lambda, lru_w_out, mla_w_in, mla_q_lat_norm, mla_w_uq, mla_kv_lat_norm, mla_w_ukv, mla_q_head_norm, mla_k_head_norm, mla_w_o, moe_w_group, moe_b_group, moe_w_router, moe_b_router, moe_w_gate, moe_w_up, moe_w_down):
    p = _prepare(mix_norm, ffn_norm, lru_w_in, lru_conv_w, lru_conv_b, lru_w_a, lru_b_a, lru_w_i, lru_b_i,
                 lru_lambda, lru_w_out, mla_w_in, mla_q_lat_norm, mla_w_uq, mla_kv_lat_norm, mla_w_ukv,
                 mla_q_head_norm, mla_k_head_norm, mla_w_o, moe_w_group, moe_b_group, moe_w_router,
                 moe_b_router, moe_w_gate, moe_w_up, moe_w_down)
    return (_trunk(x_prompt, p), _trunk(x_sample, p))
```

```python
import functools
import math

import jax
import jax.numpy as jnp
from jax import lax
from jax.experimental import pallas as pl
from jax.experimental.pallas import tpu as pltpu

F32 = jnp.float32
BF16 = jnp.bfloat16
I32 = jnp.int32

D_MODEL = 1024
LRU_WIDTH = 1024
LRU_BLOCKS = 4
LRU_BLOCK_W = LRU_WIDTH // LRU_BLOCKS
CONV_WIDTH = 4
CONV_LEFT = 2
LRU_C = 8.0
N_HEADS = 8
QK_NOPE = 128
QK_ROPE = 64
QK_HEAD = QK_NOPE + QK_ROPE
V_HEAD = 128
Q_LORA = 384
KV_LORA = 256
ROPE_THETA = 10000.0
N_GROUPS = 4
EXPERTS_PER_GROUP = 8
N_EXPERTS = N_GROUPS * EXPERTS_PER_GROUP
D_EXPERT = 512
EPS = 1e-6

LANES = 128
SUBLANES = 8
ROW_TILES = D_MODEL // LANES
HEAD_PAD = 2 * LANES
VMEM_LIMIT = 56 * 1024 * 1024
NEG_BIG = -3.0e38
Q_SCALE = math.log2(math.e) / math.sqrt(QK_HEAD)


def _cparams(sem):
    return pltpu.CompilerParams(dimension_semantics=sem, vmem_limit_bytes=VMEM_LIMIT)


def _rms(x, g):
    ms = jnp.mean(x * x, axis=-1, keepdims=True)
    return x * lax.rsqrt(ms + EPS) * g


def _dot(a, b):
    return jnp.dot(a, b, preferred_element_type=F32)


def _rows_to_tiles(ref, x):
    n = x.shape[0]
    for g in range(ROW_TILES):
        ref[pl.ds(g, n, stride=SUBLANES), :] = x[:, g * LANES:(g + 1) * LANES]


def _tiles_to_rows(ref, n):
    return jnp.concatenate([ref[pl.ds(g, n, stride=SUBLANES), :] for g in range(ROW_TILES)], axis=-1)


def _lru_in_body(xp_ref, xc_ref, xn_ref, g_ref, wg_ref, wr_ref, cw_ref, cb_ref,
                 gg_ref, xco_ref, xext_ref, rext_ref, *, ts):
    i = pl.program_id(1)
    last = pl.num_programs(1) - 1
    xext_ref[0:SUBLANES, :] = xp_ref[0]
    xext_ref[SUBLANES:SUBLANES + ts, :] = xc_ref[0]
    xext_ref[SUBLANES + ts:2 * SUBLANES + ts, :] = xn_ref[0]
    h = _rms(xext_ref[...], g_ref[...])
    gate = _dot(h[SUBLANES:SUBLANES + ts].astype(BF16), wg_ref[...])
    gg_ref[0] = jax.nn.gelu(gate, approximate=True)
    xr = _dot(h.astype(BF16), wr_ref[...])
    row = lax.broadcasted_iota(I32, (ts + 2 * SUBLANES, 1), 0)
    keep = jnp.logical_and(jnp.logical_or(row >= SUBLANES, i > 0),
                           jnp.logical_or(row < SUBLANES + ts, i < last))
    rext_ref[...] = jnp.where(keep, xr, 0.0)
    acc = jnp.broadcast_to(cb_ref[...], (ts, LRU_WIDTH))
    for k in range(CONV_WIDTH):
        acc = acc + rext_ref[pl.ds(SUBLANES - CONV_LEFT + k, ts), :] * cw_ref[k:k + 1, :]
    xco_ref[0] = acc


def _lru_in(x, g, w_gate, w_rec, conv_w, conv_b, *, ts):
    B, S, D = x.shape
    nc = S // ts
    r8 = ts // SUBLANES
    n8 = S // SUBLANES
    full = lambda shape: pl.BlockSpec(shape, lambda b, i: (0,) * len(shape))
    return pl.pallas_call(
        functools.partial(_lru_in_body, ts=ts),
        grid=(B, nc),
        in_specs=[
            pl.BlockSpec((1, SUBLANES, D), lambda b, i: (b, jnp.maximum(i * r8 - 1, 0), 0)),
            pl.BlockSpec((1, ts, D), lambda b, i: (b, i, 0)),
            pl.BlockSpec((1, SUBLANES, D), lambda b, i: (b, jnp.minimum((i + 1) * r8, n8 - 1), 0)),
            full((1, D)), full((D, LRU_WIDTH)), full((D, LRU_WIDTH)),
            full((CONV_WIDTH, LRU_WIDTH)), full((1, LRU_WIDTH)),
        ],
        out_specs=[pl.BlockSpec((1, ts, LRU_WIDTH), lambda b, i: (b, i, 0)),
                   pl.BlockSpec((1, ts, LRU_WIDTH), lambda b, i: (b, i, 0))],
        out_shape=[jax.ShapeDtypeStruct((B, S, LRU_WIDTH), F32),
                   jax.ShapeDtypeStruct((B, S, LRU_WIDTH), F32)],
        scratch_shapes=[pltpu.VMEM((ts + 2 * SUBLANES, D), F32),
                        pltpu.VMEM((ts + 2 * SUBLANES, LRU_WIDTH), F32)],
        compiler_params=_cparams(("parallel", "parallel")),
        name="lru_in",
    )(x, x, x, g, w_gate, w_rec, conv_w, conv_b)


def _lru_scan_body(*refs, ts, nb, reverse, final):
    if final:
        xc_ref, wa_ref, ba_ref, wi_ref, bi_ref, lam_ref, hf_ref, gg_ref, out_ref = refs[:9]
        scr = refs[9:]
    else:
        xc_ref, wa_ref, ba_ref, wi_ref, bi_ref, lam_ref, out_ref = refs[:7]
        scr = refs[7:]
    a_bufs, u_bufs, h_bufs = scr[0:nb], scr[nb:2 * nb], scr[2 * nb:3 * nb]
    carry_ref = scr[3 * nb]
    pitch = ts + SUBLANES
    groups = LRU_WIDTH // LANES

    @pl.when(pl.program_id(1) == 0)
    def _():
        carry_ref[...] = jnp.zeros_like(carry_ref)

    neg_c_sp = -LRU_C * jax.nn.softplus(-lam_ref[...])
    for bb in range(nb):
        xc = xc_ref[bb]
        xb = xc.astype(BF16)

        def block_diag(w_ref, b_ref):
            parts = [_dot(xb[:, n * LRU_BLOCK_W:(n + 1) * LRU_BLOCK_W], w_ref[n])
                     for n in range(LRU_BLOCKS)]
            return jnp.concatenate(parts, axis=-1) + b_ref[...]

        r = jax.nn.sigmoid(block_diag(wa_ref, ba_ref))
        ig = jax.nn.sigmoid(block_diag(wi_ref, bi_ref))
        log_a = neg_c_sp * r
        a = jnp.exp(log_a)
        u = jnp.sqrt(1.0 - a * a) * (ig * xc)
        for g in range(groups):
            a_bufs[bb][pl.ds(g * pitch, ts), :] = a[:, g * LANES:(g + 1) * LANES]
            u_bufs[bb][pl.ds(g * pitch, ts), :] = u[:, g * LANES:(g + 1) * LANES]

    def step(t, hs):
        tt = (ts - 1 - t) if reverse else t
        new = []
        for bb in range(nb):
            a_t = a_bufs[bb][pl.ds(tt, groups, stride=pitch), :]
            u_t = u_bufs[bb][pl.ds(tt, groups, stride=pitch), :]
            h = a_t * hs[bb] + u_t
            h_bufs[bb][pl.ds(tt, groups, stride=pitch), :] = h
            new.append(h)
        return tuple(new)

    hs = lax.fori_loop(0, ts, step, tuple(carry_ref[bb] for bb in range(nb)), unroll=8)
    for bb in range(nb):
        carry_ref[bb] = hs[bb]
        h_full = jnp.concatenate([h_bufs[bb][pl.ds(g * pitch, ts), :] for g in range(groups)], axis=-1)
        if final:
            out_ref[bb] = ((hf_ref[bb] + h_full) * gg_ref[bb]).astype(BF16)
        else:
            out_ref[bb] = h_full


def _lru_scan(xc, w_a, b_a, w_i, b_i, lam, hf=None, gg=None, *, ts, nb, reverse):
    B, S, W = xc.shape
    nc = S // ts
    final = hf is not None
    pitch = ts + SUBLANES
    cidx = (lambda b, i: (b, nc - 1 - i, 0)) if reverse else (lambda b, i: (b, i, 0))
    full = lambda shape: pl.BlockSpec(shape, lambda b, i: (0,) * len(shape))
    blk = pl.BlockSpec((nb, ts, W), cidx)
    in_specs = [blk, full((LRU_BLOCKS, LRU_BLOCK_W, LRU_BLOCK_W)), full((1, W)),
                full((LRU_BLOCKS, LRU_BLOCK_W, LRU_BLOCK_W)), full((1, W)), full((1, W))]
    args = [xc, w_a, b_a, w_i, b_i, lam]
    if final:
        in_specs += [blk, blk]
        args += [hf, gg]
    slab = pltpu.VMEM((W // LANES * pitch, LANES), F32)
    return pl.pallas_call(
        functools.partial(_lru_scan_body, ts=ts, nb=nb, reverse=reverse, final=final),
        grid=(B // nb, nc),
        in_specs=in_specs,
        out_specs=pl.BlockSpec((nb, ts, W), cidx),
        out_shape=jax.ShapeDtypeStruct((B, S, W), BF16 if final else F32),
        scratch_shapes=[slab] * (3 * nb) + [pltpu.VMEM((nb, W // LANES, LANES), F32)],
        compiler_params=_cparams(("parallel", "arbitrary")),
        name="lru_scan_bwd" if reverse else "lru_scan_fwd",
    )(*args)


def _post_mixer_body(x_ref, y_ref, w_ref, g_ref, wrh_ref, wrl_ref, br_ref,
                     x1_ref, xn_ref, mi_ref, mw_ref, cnt_ref, tri_ref):
    x1 = x_ref[...] + _dot(y_ref[...], w_ref[...])
    x1_ref[...] = x1
    xn = _rms(x1, g_ref[...])
    tm = xn.shape[0]
    _rows_to_tiles(xn_ref, xn)

    @pl.when(pl.program_id(0) == 0)
    def _():
        cnt_ref[...] = jnp.zeros_like(cnt_ref)
        r = lax.broadcasted_iota(I32, (tm, tm), 0)
        c = lax.broadcasted_iota(I32, (tm, tm), 1)
        tri_ref[...] = jnp.where(c < r, 1.0, 0.0).astype(BF16)

    xh = xn.astype(BF16)
    xl = (xn - xh.astype(F32)).astype(BF16)
    lg = _dot(xh, wrh_ref[...]) + (_dot(xl, wrh_ref[...]) + _dot(xh, wrl_ref[...])) + br_ref[...]
    lane = lax.broadcasted_iota(I32, (tm, LANES), 1)
    is_group = jnp.logical_and(lane >= N_EXPERTS, lane < N_EXPERTS + N_GROUPS)
    glog = jnp.where(is_group, lg, NEG_BIG)
    gmax = jnp.max(glog, axis=-1, keepdims=True)
    gsum = jnp.sum(jnp.where(is_group, jnp.exp(glog - gmax), 0.0), axis=-1, keepdims=True)
    g_top_p = 1.0 / gsum
    g_top = jnp.min(jnp.where(glog == gmax, lane, 4 * LANES), axis=-1, keepdims=True) - N_EXPERTS
    in_group = jnp.logical_and(lane < N_EXPERTS, (lane // EXPERTS_PER_GROUP) == g_top)
    sel = jnp.where(in_group, lg, NEG_BIG)
    v1 = jnp.max(sel, axis=-1, keepdims=True)
    i1 = jnp.min(jnp.where(sel == v1, lane, 4 * LANES), axis=-1, keepdims=True)
    sel2 = jnp.where(lane == i1, NEG_BIG, sel)
    v2 = jnp.max(sel2, axis=-1, keepdims=True)
    i2 = jnp.min(jnp.where(sel2 == v2, lane, 4 * LANES), axis=-1, keepdims=True)
    e2 = jnp.exp(v2 - v1)
    den = 1.0 + e2
    w1 = (1.0 / den) * g_top_p
    w2 = (e2 / den) * g_top_p
    hot1 = (lane == i1).astype(F32)
    hot2 = (lane == i2).astype(F32)
    hot = hot1 + hot2
    before = _dot(tri_ref[...], hot.astype(BF16)) + cnt_ref[...]
    r1 = jnp.sum(hot1 * before, axis=-1, keepdims=True).astype(I32)
    r2 = jnp.sum(hot2 * before, axis=-1, keepdims=True).astype(I32)
    cnt_ref[...] = cnt_ref[...] + jnp.sum(hot, axis=0, keepdims=True)
    mi_ref[...] = jnp.where(lane == 0, i1, jnp.where(lane == 1, i2,
                            jnp.where(lane == 2, r1, jnp.where(lane == 3, r2, 0))))
    mw_ref[...] = jnp.where(lane == 0, w1, jnp.where(lane == 1, w2, 0.0))


def _post_mixer(x, y, w, g, wr_hi, wr_lo, br, *, tm):
    T, D = x.shape
    K = y.shape[1]
    full = lambda shape: pl.BlockSpec(shape, lambda i: (0,) * len(shape))
    row = lambda n: pl.BlockSpec((tm, n), lambda i: (i, 0))
    return pl.pallas_call(
        _post_mixer_body,
        grid=(T // tm,),
        in_specs=[row(D), row(K), full((K, D)), full((1, D)),
                  full((D, LANES)), full((D, LANES)), full((1, LANES))],
        out_specs=[row(D), pl.BlockSpec((tm * SUBLANES, LANES), lambda i: (i, 0)),
                   row(LANES), row(LANES), full((1, LANES))],
        out_shape=[jax.ShapeDtypeStruct((T, D), F32), jax.ShapeDtypeStruct((T * SUBLANES, LANES), F32),
                   jax.ShapeDtypeStruct((T, LANES), I32), jax.ShapeDtypeStruct((T, LANES), F32),
                   jax.ShapeDtypeStruct((1, LANES), F32)],
        scratch_shapes=[pltpu.VMEM((tm, tm), BF16)],
        compiler_params=_cparams(("arbitrary",)),
        name="post_mixer",
    )(x, y, w, g, wr_hi, wr_lo, br)


def _route_tables(mi, cnt, *, te):
    T = mi.shape[0]
    nt = (2 * T) // te + N_EXPERTS
    counts = cnt[0, :N_EXPERTS].astype(I32)
    padded = ((counts + te - 1) // te) * te
    pend = jnp.cumsum(padded)
    pstart = pend - padded
    eids = jnp.arange(N_EXPERTS, dtype=I32)
    start_of = lambda e: jnp.sum(jnp.where(e[..., None] == eids, pstart, 0), axis=-1)
    pos = jnp.stack([start_of(mi[:, 0]) + mi[:, 2], start_of(mi[:, 1]) + mi[:, 3]], axis=-1).reshape(-1)
    n_used = pend[N_EXPERTS - 1] // te
    tile_start = jnp.minimum(jnp.arange(nt, dtype=I32), n_used - 1) * te
    tile_e = jnp.sum((tile_start[:, None] >= pend[None, :]).astype(I32), axis=-1)
    return (pos.astype(I32), tile_e.astype(I32), n_used.reshape(1).astype(I32),
            (pstart + counts).astype(I32), (padded - counts).astype(I32))


def _dispatch_body(pos_ref, nu_ref, ps_ref, pl_ref, xn_ref, xs_hbm, zbuf, sem, zsem, *, tm, te, nt):
    j = pl.program_id(0)
    base = j * (2 * tm)

    def body(r, c):
        for k in range(2):
            p = pos_ref[base + 2 * r + k]
            pltpu.make_async_copy(xn_ref.at[r], xs_hbm.at[p], sem).start()
        return c

    lax.fori_loop(0, tm, body, 0, unroll=8)

    @pl.when(j == pl.num_programs(0) - 1)
    def _():
        zbuf[...] = jnp.zeros_like(zbuf)
        nu = nu_ref[0]

        def pad_copy(e):
            n = pl_ref[e]
            return pltpu.make_async_copy(zbuf.at[pl.ds(0, n)], xs_hbm.at[pl.ds(ps_ref[e], n)], zsem)

        def tail_copy(t):
            return pltpu.make_async_copy(zbuf, xs_hbm.at[pl.ds(t * te, te)], zsem)

        def pads(e, c, start):
            @pl.when(pl_ref[e] > 0)
            def _():
                pad_copy(e).start() if start else pad_copy(e).wait()
            return c

        def tails(t, c, start):
            tail_copy(t).start() if start else tail_copy(t).wait()
            return c

        lax.fori_loop(0, N_EXPERTS, functools.partial(pads, start=True), 0)
        lax.fori_loop(nu, nt, functools.partial(tails, start=True), 0)
        lax.fori_loop(0, N_EXPERTS, functools.partial(pads, start=False), 0)
        lax.fori_loop(nu, nt, functools.partial(tails, start=False), 0)

    for k in range(2):
        pltpu.make_async_copy(xn_ref, xs_hbm.at[pl.ds(0, tm)], sem).wait()


def _dispatch(xn3, pos, n_used, pad_start, pad_len, *, tm, te):
    T = xn3.shape[0]
    nt = (2 * T) // te + N_EXPERTS
    grid_spec = pltpu.PrefetchScalarGridSpec(
        num_scalar_prefetch=4,
        grid=(T // tm,),
        in_specs=[pl.BlockSpec((tm, SUBLANES, LANES), lambda j, *_: (j, 0, 0))],
        out_specs=pl.BlockSpec(memory_space=pl.ANY),
        scratch_shapes=[pltpu.VMEM((te, SUBLANES, LANES), F32),
                        pltpu.SemaphoreType.DMA, pltpu.SemaphoreType.DMA],
    )
    return pl.pallas_call(
        functools.partial(_dispatch_body, tm=tm, te=te, nt=nt),
        grid_spec=grid_spec,
        out_shape=jax.ShapeDtypeStruct((nt * te, SUBLANES, LANES), F32),
        compiler_params=_cparams(("arbitrary",)),
        name="moe_dispatch",
    )(pos, n_used, pad_start, pad_len, xn3)


def _experts_body(te_ref, nu_ref, xs_ref, wg_ref, wu_ref, wd_ref, os_ref, wgb, wub, wdb, *, te):
    j = pl.program_id(0)
    nu = nu_ref[0]

    @pl.when(j < nu)
    def _():
        changed = jnp.logical_or(j == 0, te_ref[j] != te_ref[jnp.maximum(j - 1, 0)])

        @pl.when(changed)
        def _():
            wgb[...] = wg_ref[0].astype(BF16)
            wub[...] = wu_ref[0].astype(BF16)
            wdb[...] = wd_ref[0].astype(BF16)

        x = _tiles_to_rows(xs_ref, te).astype(BF16)
        h = (jax.nn.silu(_dot(x, wgb[...])) * _dot(x, wub[...])).astype(BF16)
        _rows_to_tiles(os_ref, _dot(h, wdb[...]))

    @pl.when(j >= nu)
    def _():
        os_ref[...] = jnp.zeros_like(os_ref)


def _experts(xs2, tile_e, n_used, w_gate, w_up, w_down, *, te):
    nt = xs2.shape[0] // (te * SUBLANES)
    D = D_MODEL
    wspec = lambda shape: pl.BlockSpec((1,) + shape, lambda j, tile_e, nu: (tile_e[j], 0, 0))
    grid_spec = pltpu.PrefetchScalarGridSpec(
        num_scalar_prefetch=2,
        grid=(nt,),
        in_specs=[pl.BlockSpec((te * SUBLANES, LANES), lambda j, tile_e, nu: (jnp.minimum(j, nu[0] - 1), 0)),
                  wspec((D, D_EXPERT)), wspec((D, D_EXPERT)), wspec((D_EXPERT, D))],
        out_specs=pl.BlockSpec((te * SUBLANES, LANES), lambda j, tile_e, nu: (j, 0)),
        scratch_shapes=[pltpu.VMEM((D, D_EXPERT), BF16), pltpu.VMEM((D, D_EXPERT), BF16),
                        pltpu.VMEM((D_EXPERT, D), BF16)],
    )
    return pl.pallas_call(
        functools.partial(_experts_body, te=te),
        grid_spec=grid_spec,
        out_shape=jax.ShapeDtypeStruct(xs2.shape, F32),
        compiler_params=_cparams(("arbitrary",)),
        name="moe_experts",
    )(tile_e, n_used, xs2, w_gate, w_up, w_down)


def _combine_body(pos_ref, x1_ref, mw_ref, os_hbm, y_ref, g0, g1, sem, *, tm):
    j = pl.program_id(0)
    base = j * (2 * tm)
    bufs = (g0, g1)

    def tile_rows(i):
        return pl.ds(pl.multiple_of(i * SUBLANES, SUBLANES), SUBLANES)

    def body(r, c):
        for k in range(2):
            p = pos_ref[base + 2 * r + k]
            pltpu.make_async_copy(os_hbm.at[tile_rows(p), :], bufs[k].at[tile_rows(r), :], sem).start()
        return c

    lax.fori_loop(0, tm, body, 0, unroll=8)
    for k in range(2):
        pltpu.make_async_copy(os_hbm.at[pl.ds(0, tm * SUBLANES), :], bufs[k], sem).wait()
    w = mw_ref[...]
    y_ref[...] = x1_ref[...] + (w[:, 0:1] * _tiles_to_rows(g0, tm) + w[:, 1:2] * _tiles_to_rows(g1, tm))


def _combine(x1, mw, os3, pos, *, tm):
    T, D = x1.shape
    grid_spec = pltpu.PrefetchScalarGridSpec(
        num_scalar_prefetch=1,
        grid=(T // tm,),
        in_specs=[pl.BlockSpec((tm, D), lambda j, pos: (j, 0)),
                  pl.BlockSpec((tm, LANES), lambda j, pos: (j, 0)),
                  pl.BlockSpec(memory_space=pl.ANY)],
        out_specs=pl.BlockSpec((tm, D), lambda j, pos: (j, 0)),
        scratch_shapes=[pltpu.VMEM((tm * SUBLANES, LANES), F32), pltpu.VMEM((tm * SUBLANES, LANES), F32),
                        pltpu.SemaphoreType.DMA],
    )
    return pl.pallas_call(
        functools.partial(_combine_body, tm=tm),
        grid_spec=grid_spec,
        out_shape=jax.ShapeDtypeStruct((T, D), F32),
        compiler_params=_cparams(("arbitrary",)),
        name="moe_combine",
    )(pos, x1, mw, os3)


def _mla_proj_body(x_ref, g_ref, win_ref, gq_ref, wuq_ref, gkv_ref, wuk_ref, wuv_ref,
                   gqh_ref, gkh_ref, cos_ref, sa_ref, sb_ref, q_ref, k_ref, v_ref):
    h = _rms(x_ref[0], g_ref[...]).astype(BF16)
    c = _dot(h, win_ref[...])
    cq = _rms(c[:, :Q_LORA], gq_ref[...]).astype(BF16)
    ckv = _rms(c[:, Q_LORA:Q_LORA + KV_LORA], gkv_ref[...]).astype(BF16)
    kpe = c[:, Q_LORA + KV_LORA:]
    q = _dot(cq, wuq_ref[...])
    kn = _dot(ckv, wuk_ref[...])
    v_ref[0] = _dot(ckv, wuv_ref[...]).astype(BF16)
    cos_t, sin_a, sin_b = cos_ref[...], sa_ref[...], sb_ref[...]
    gqh, gkh = gqh_ref[...], gkh_ref[...]

    def rope(r):
        return r * cos_t + pltpu.roll(r, LANES - QK_ROPE // 2, 1) * sin_a + pltpu.roll(r, QK_ROPE // 2, 1) * sin_b

    kpe_ss = jnp.sum(kpe * kpe, axis=-1, keepdims=True)
    for hh in range(N_HEADS):
        lo = hh * HEAD_PAD
        qn = q[:, lo:lo + LANES]
        qr = q[:, lo + LANES:lo + HEAD_PAD]
        ms = (jnp.sum(qn * qn, axis=-1, keepdims=True) + jnp.sum(qr * qr, axis=-1, keepdims=True)) / QK_HEAD
        sc = lax.rsqrt(ms + EPS) * Q_SCALE
        q_ref[0, :, lo:lo + LANES] = (qn * sc * gqh[:, :LANES]).astype(BF16)
        q_ref[0, :, lo + LANES:lo + HEAD_PAD] = rope(qr * sc * gqh[:, LANES:]).astype(BF16)
        kh = kn[:, hh * QK_NOPE:(hh + 1) * QK_NOPE]
        ms = (jnp.sum(kh * kh, axis=-1, keepdims=True) + kpe_ss) / QK_HEAD
        sc = lax.rsqrt(ms + EPS)
        k_ref[0, :, lo:lo + LANES] = (kh * sc * gkh[:, :LANES]).astype(BF16)
        k_ref[0, :, lo + LANES:lo + HEAD_PAD] = rope(kpe * sc * gkh[:, LANES:]).astype(BF16)


def _mla_proj(x, g, w_in, gq, w_uq, gkv, w_uk, w_uv, gqh, gkh, cos_t, sin_a, sin_b, *, tm):
    B, S, D = x.shape
    full = lambda shape: pl.BlockSpec(shape, lambda b, i: (0,) * len(shape))
    pos = pl.BlockSpec((tm, LANES), lambda b, i: (i, 0))
    HP = N_HEADS * HEAD_PAD
    HV = N_HEADS * V_HEAD
    return pl.pallas_call(
        _mla_proj_body,
        grid=(B, S // tm),
        in_specs=[pl.BlockSpec((1, tm, D), lambda b, i: (b, i, 0)), full((1, D)),
                  full(w_in.shape), full((1, Q_LORA)), full(w_uq.shape), full((1, KV_LORA)),
                  full(w_uk.shape), full(w_uv.shape), full((1, HEAD_PAD)), full((1, HEAD_PAD)),
                  pos, pos, pos],
        out_specs=[pl.BlockSpec((1, tm, HP), lambda b, i: (b, i, 0)),
                   pl.BlockSpec((1, tm, HP), lambda b, i: (b, i, 0)),
                   pl.BlockSpec((1, tm, HV), lambda b, i: (b, i, 0))],
        out_shape=[jax.ShapeDtypeStruct((B, S, HP), BF16), jax.ShapeDtypeStruct((B, S, HP), BF16),
                   jax.ShapeDtypeStruct((B, S, HV), BF16)],
        compiler_params=_cparams(("parallel", "parallel")),
        name="mla_proj",
    )(x, g, w_in, gq, w_uq, gkv, w_uk, w_uv, gqh, gkh, cos_t, sin_a, sin_b)


def _attn_body(q_ref, k_ref, v_ref, o_ref, *s_refs, tk, nkv):
    q = q_ref[0]
    tq = q.shape[0]

    def scores(c):
        kc = k_ref[0, pl.ds(pl.multiple_of(c * tk, tk), tk), :]
        return lax.dot_general(kc, q, (((1,), (1,)), ((), ())), preferred_element_type=F32)

    def update(s_ref, c, carry):
        m, l, acc = carry
        s = s_ref[...]
        m_new = jnp.maximum(m, jnp.max(s, axis=0, keepdims=True))
        alpha = jnp.exp2(m - m_new)
        p = jnp.exp2(s - m_new)
        l = alpha * l + jnp.sum(p, axis=0, keepdims=True)
        vc = v_ref[0, pl.ds(pl.multiple_of(c * tk, tk), tk), :]
        pv = lax.dot_general(vc, p.astype(BF16), (((0,), (0,)), ((), ())), preferred_element_type=F32)
        return m_new, l, alpha * acc + pv

    nbuf = len(s_refs)
    s_refs[0][...] = scores(0)

    def group(i, carry):
        c0 = nbuf * i
        for u in range(nbuf):
            s_refs[(u + 1) % nbuf][...] = scores(jnp.minimum(c0 + u + 1, nkv - 1))
            carry = update(s_refs[u], c0 + u, carry)
        return carry

    init = (jnp.full((1, tq), -jnp.inf, F32), jnp.zeros((1, tq), F32), jnp.zeros((V_HEAD, tq), F32))
    _, l, acc = lax.fori_loop(0, nkv // nbuf, group, init)
    o_ref[0] = jnp.transpose(acc / l).astype(BF16)


def _attention(q, k, v, *, tq, tk, nbuf):
    B, S, _ = q.shape
    nkv = S // tk
    assert nkv % nbuf == 0, (S, tk, nbuf)
    return pl.pallas_call(
        functools.partial(_attn_body, tk=tk, nkv=nkv),
        grid=(B, N_HEADS, S // tq),
        in_specs=[pl.BlockSpec((1, tq, HEAD_PAD), lambda b, h, i: (b, i, h)),
                  pl.BlockSpec((1, S, HEAD_PAD), lambda b, h, i: (b, 0, h)),
                  pl.BlockSpec((1, S, V_HEAD), lambda b, h, i: (b, 0, h))],
        out_specs=pl.BlockSpec((1, tq, V_HEAD), lambda b, h, i: (b, i, h)),
        out_shape=jax.ShapeDtypeStruct((B, S, N_HEADS * V_HEAD), BF16),
        scratch_shapes=[pltpu.VMEM((tk, tq), F32)] * nbuf,
        compiler_params=_cparams(("parallel", "parallel", "arbitrary")),
        name="mla_attention",
    )(q, k, v)


def _rope_tables(S):
    inv = 1.0 / (ROPE_THETA ** (jnp.arange(0, QK_ROPE, 2, dtype=F32) / QK_ROPE))
    ang = jnp.arange(S, dtype=F32)[:, None] * inv[None, :]
    cos, sin = jnp.cos(ang), jnp.sin(ang)
    z = jnp.zeros_like(cos)
    cos_t = jnp.concatenate([cos, cos, z, z], axis=-1)
    sin_a = jnp.concatenate([-sin, z, z, z], axis=-1)
    sin_b = jnp.concatenate([z, sin, z, z], axis=-1)
    return cos_t, sin_a, sin_b


def _head_pad_cols(w_nope, w_rope):
    K = w_nope.shape[0]
    z = jnp.zeros((K, N_HEADS, HEAD_PAD - QK_HEAD), w_nope.dtype)
    return jnp.concatenate([w_nope, w_rope, z], axis=-1).reshape(K, N_HEADS * HEAD_PAD)


def _router_params(w_group, b_group, w_router, b_router):
    pad = LANES - N_EXPERTS - N_GROUPS
    w = jnp.concatenate([w_router, w_group, jnp.zeros((D_MODEL, pad), F32)], axis=-1)
    b = jnp.concatenate([b_router, b_group, jnp.zeros((pad,), F32)]).reshape(1, LANES)
    w_hi = w.astype(BF16)
    w_lo = (w - w_hi.astype(F32)).astype(BF16)
    return w_hi, w_lo, b


def _tile(n, pref):
    t = min(n, pref)
    assert n % t == 0, (n, t)
    return t


def _moe(x, y, w_proj, g, router, w_gate, w_up, w_down, *, final_shape):
    T = x.shape[0]
    tm = _tile(T, 512)
    te = _tile(T, 256)
    td = _tile(T, 256)
    x1, xn2, mi, mw, cnt = _post_mixer(x, y, w_proj, g, *router, tm=tm)
    pos, tile_e, n_used, pad_start, pad_len = _route_tables(mi, cnt, te=te)
    xs3 = _dispatch(xn2.reshape(T, SUBLANES, LANES), pos, n_used, pad_start, pad_len, tm=td, te=te)
    os2 = _experts(xs3.reshape(-1, LANES), tile_e, n_used, w_gate, w_up, w_down, te=te)
    return _combine(x1, mw, os2, pos, tm=td).reshape(final_shape)


def _trunk(x, p):
    B, S, D = x.shape
    T = B * S
    ts = _tile(S, 256)
    nb = 2 if B % 2 == 0 else 1
    gg, xc = _lru_in(x, p["mix_g"][0], p["lru_w_gate"], p["lru_w_rec"], p["conv_w"], p["conv_b"], ts=ts)
    hf = _lru_scan(xc, p["w_a"][0], p["b_a"][0], p["w_i"][0], p["b_i"][0], p["lam"][0],
                   ts=ts, nb=nb, reverse=False)
    y = _lru_scan(xc, p["w_a"][1], p["b_a"][1], p["w_i"][1], p["b_i"][1], p["lam"][1], hf, gg,
                  ts=ts, nb=nb, reverse=True)
    x = _moe(x.reshape(T, D), y.reshape(T, LRU_WIDTH), p["lru_w_out"], p["ffn_g"][0], p["router"][0],
             p["moe_w_gate"][0], p["moe_w_up"][0], p["moe_w_down"][0], final_shape=(B, S, D))
    tm = _tile(S, 256)
    cos_t, sin_a, sin_b = _rope_tables(S)
    q, k, v = _mla_proj(x, p["mix_g"][1], p["mla_w_in"], p["gq"], p["w_uq"], p["gkv"], p["w_uk"], p["w_uv"],
                        p["gqh"], p["gkh"], cos_t, sin_a, sin_b, tm=tm)
    tk = _tile(S, 512)
    o = _attention(q, k, v, tq=_tile(S, 512), tk=tk, nbuf=min(4, S // tk))
    x = _moe(x.reshape(T, D), o.reshape(T, N_HEADS * V_HEAD), p["mla_w_o"], p["ffn_g"][1], p["router"][1],
             p["moe_w_gate"][1], p["moe_w_up"][1], p["moe_w_down"][1], final_shape=(B, S, D))
    return x


def _prepare(mix_norm, ffn_norm, lru_w_in, lru_conv_w, lru_conv_b, lru_w_a, lru_b_a, lru_w_i, lru_b_i,
             lru_lambda, lru_w_out, mla_w_in, mla_q_lat_norm, mla_w_uq, mla_kv_lat_norm, mla_w_ukv,
             mla_q_head_norm, mla_k_head_norm, mla_w_o, moe_w_group, moe_b_group, moe_w_router,
             moe_b_router, moe_w_gate, moe_w_up, moe_w_down):
    W = LRU_WIDTH
    w_uq = mla_w_uq[0].reshape(Q_LORA, N_HEADS, QK_HEAD)
    w_ukv = mla_w_ukv[0].reshape(KV_LORA, N_HEADS, QK_NOPE + V_HEAD)
    head_gain = lambda g: jnp.concatenate([g, jnp.zeros((HEAD_PAD - QK_HEAD,), F32)]).reshape(1, HEAD_PAD)
    w_in = jnp.concatenate([mla_w_in[0], jnp.zeros((D_MODEL, LANES - QK_ROPE), F32)], axis=-1)
    return {
        "mix_g": mix_norm.reshape(-1, 1, D_MODEL),
        "ffn_g": ffn_norm.reshape(-1, 1, D_MODEL),
        "lru_w_gate": lru_w_in[0, :, :W].astype(BF16),
        "lru_w_rec": lru_w_in[0, :, W:].astype(BF16),
        "conv_w": lru_conv_w[0],
        "conv_b": lru_conv_b[0].reshape(1, W),
        "w_a": lru_w_a[0].astype(BF16),
        "b_a": lru_b_a[0].reshape(2, 1, W),
        "w_i": lru_w_i[0].astype(BF16),
        "b_i": lru_b_i[0].reshape(2, 1, W),
        "lam": lru_lambda[0].reshape(2, 1, W),
        "lru_w_out": lru_w_out[0].astype(BF16),
        "mla_w_in": w_in.astype(BF16),
        "gq": mla_q_lat_norm[0].reshape(1, Q_LORA),
        "w_uq": _head_pad_cols(w_uq[:, :, :QK_NOPE], w_uq[:, :, QK_NOPE:]).astype(BF16),
        "gkv": mla_kv_lat_norm[0].reshape(1, KV_LORA),
        "w_uk": w_ukv[:, :, :QK_NOPE].reshape(KV_LORA, N_HEADS * QK_NOPE).astype(BF16),
        "w_uv": w_ukv[:, :, QK_NOPE:].reshape(KV_LORA, N_HEADS * V_HEAD).astype(BF16),
        "gqh": head_gain(mla_q_head_norm[0]),
        "gkh": head_gain(mla_k_head_norm[0]),
        "mla_w_o": mla_w_o[0].astype(BF16),
        "router": [_router_params(moe_w_group[l], moe_b_group[l], moe_w_router[l], moe_b_router[l])
                   for l in range(2)],
        "moe_w_gate": moe_w_gate, "moe_w_up": moe_w_up, "moe_w_down": moe_w_down,
    }


def kernel(x_prompt, x_sample, mix_norm, ffn_norm, lru_w_in, lru_conv_w, lru_conv_b, lru_w_a, lru_b_a, lru_w_i, lru_b_i, lru_lambda, lru_w_out, mla_w_in, mla_q_lat_norm, mla_w_uq, mla_kv_lat_norm, mla_w_ukv, mla_q_head_norm, mla_k_head_norm, mla_w_o, moe_w_group, moe_b_group, moe_w_router, moe_b_router, moe_w_gate, moe_w_up, moe_w_down):
    p = _prepare(mix_norm, ffn_norm, lru_w_in, lru_conv_w, lru_conv_b, lru_w_a, lru_b_a, lru_w_i, lru_b_i,
                 lru_lambda, lru_w_out, mla_w_in, mla_q_lat_norm, mla_w_uq, mla_kv_lat_norm, mla_w_ukv,
                 mla_q_head_norm, mla_k_head_norm, mla_w_o, moe_w_group, moe_b_group, moe_w_router,
                 moe_b_router, moe_w_gate, moe_w_up, moe_w_down)
    return (_trunk(x_prompt, p), _trunk(x_sample, p))
```

```python
import functools
import math

import jax
import jax.numpy as jnp
from jax import lax
from jax.experimental import pallas as pl
from jax.experimental.pallas import tpu as pltpu

F32 = jnp.float32
BF16 = jnp.bfloat16
I32 = jnp.int32

D_MODEL = 1024
LRU_WIDTH = 1024
LRU_BLOCKS = 4
LRU_BLOCK_W = LRU_WIDTH // LRU_BLOCKS
CONV_WIDTH = 4
CONV_LEFT = 2
LRU_C = 8.0
N_HEADS = 8
QK_NOPE = 128
QK_ROPE = 64
QK_HEAD = QK_NOPE + QK_ROPE
V_HEAD = 128
Q_LORA = 384
KV_LORA = 256
ROPE_THETA = 10000.0
N_GROUPS = 4
EXPERTS_PER_GROUP = 8
N_EXPERTS = N_GROUPS * EXPERTS_PER_GROUP
D_EXPERT = 512
EPS = 1e-6

LANES = 128
SUBLANES = 8
ROW_TILES = D_MODEL // LANES
HEAD_PAD = 2 * LANES
VMEM_LIMIT = 56 * 1024 * 1024
NEG_BIG = -3.0e38
Q_SCALE = math.log2(math.e) / math.sqrt(QK_HEAD)


def _cparams(sem):
    return pltpu.CompilerParams(dimension_semantics=sem, vmem_limit_bytes=VMEM_LIMIT)


def _rms(x, g):
    ms = jnp.mean(x * x, axis=-1, keepdims=True)
    return x * lax.rsqrt(ms + EPS) * g


def _dot(a, b):
    return jnp.dot(a, b, preferred_element_type=F32)


def _rows_to_tiles(ref, x):
    n = x.shape[0]
    for g in range(ROW_TILES):
        ref[pl.ds(g, n, stride=SUBLANES), :] = x[:, g * LANES:(g + 1) * LANES]


def _tiles_to_rows(ref, n):
    return jnp.concatenate([ref[pl.ds(g, n, stride=SUBLANES), :] for g in range(ROW_TILES)], axis=-1)


def _lru_in_body(xp_ref, xc_ref, xn_ref, g_ref, wg_ref, wr_ref, cw_ref, cb_ref,
                 gg_ref, xco_ref, xext_ref, rext_ref, *, ts):
    i = pl.program_id(1)
    last = pl.num_programs(1) - 1
    xext_ref[0:SUBLANES, :] = xp_ref[0]
    xext_ref[SUBLANES:SUBLANES + ts, :] = xc_ref[0]
    xext_ref[SUBLANES + ts:2 * SUBLANES + ts, :] = xn_ref[0]
    h = _rms(xext_ref[...], g_ref[...])
    gate = _dot(h[SUBLANES:SUBLANES + ts].astype(BF16), wg_ref[...])
    gg_ref[0] = jax.nn.gelu(gate, approximate=True)
    xr = _dot(h.astype(BF16), wr_ref[...])
    row = lax.broadcasted_iota(I32, (ts + 2 * SUBLANES, 1), 0)
    keep = jnp.logical_and(jnp.logical_or(row >= SUBLANES, i > 0),
                           jnp.logical_or(row < SUBLANES + ts, i < last))
    rext_ref[...] = jnp.where(keep, xr, 0.0)
    acc = jnp.broadcast_to(cb_ref[...], (ts, LRU_WIDTH))
    for k in range(CONV_WIDTH):
        acc = acc + rext_ref[pl.ds(SUBLANES - CONV_LEFT + k, ts), :] * cw_ref[k:k + 1, :]
    xco_ref[0] = acc


def _lru_in(x, g, w_gate, w_rec, conv_w, conv_b, *, ts):
    B, S, D = x.shape
    nc = S // ts
    r8 = ts // SUBLANES
    n8 = S // SUBLANES
    full = lambda shape: pl.BlockSpec(shape, lambda b, i: (0,) * len(shape))
    return pl.pallas_call(
        functools.partial(_lru_in_body, ts=ts),
        grid=(B, nc),
        in_specs=[
            pl.BlockSpec((1, SUBLANES, D), lambda b, i: (b, jnp.maximum(i * r8 - 1, 0), 0)),
            pl.BlockSpec((1, ts, D), lambda b, i: (b, i, 0)),
            pl.BlockSpec((1, SUBLANES, D), lambda b, i: (b, jnp.minimum((i + 1) * r8, n8 - 1), 0)),
            full((1, D)), full((D, LRU_WIDTH)), full((D, LRU_WIDTH)),
            full((CONV_WIDTH, LRU_WIDTH)), full((1, LRU_WIDTH)),
        ],
        out_specs=[pl.BlockSpec((1, ts, LRU_WIDTH), lambda b, i: (b, i, 0)),
                   pl.BlockSpec((1, ts, LRU_WIDTH), lambda b, i: (b, i, 0))],
        out_shape=[jax.ShapeDtypeStruct((B, S, LRU_WIDTH), F32),
                   jax.ShapeDtypeStruct((B, S, LRU_WIDTH), F32)],
        scratch_shapes=[pltpu.VMEM((ts + 2 * SUBLANES, D), F32),
                        pltpu.VMEM((ts + 2 * SUBLANES, LRU_WIDTH), F32)],
        compiler_params=_cparams(("parallel", "parallel")),
        name="lru_in",
    )(x, x, x, g, w_gate, w_rec, conv_w, conv_b)


def _lru_scan_body(*refs, ts, nb, reverse, final):
    if final:
        xc_ref, wa_ref, ba_ref, wi_ref, bi_ref, lam_ref, hf_ref, gg_ref, out_ref = refs[:9]
        scr = refs[9:]
    else:
        xc_ref, wa_ref, ba_ref, wi_ref, bi_ref, lam_ref, out_ref = refs[:7]
        scr = refs[7:]
    a_bufs, u_bufs, h_bufs = scr[0:nb], scr[nb:2 * nb], scr[2 * nb:3 * nb]
    carry_ref = scr[3 * nb]
    pitch = ts + SUBLANES
    groups = LRU_WIDTH // LANES

    @pl.when(pl.program_id(1) == 0)
    def _():
        carry_ref[...] = jnp.zeros_like(carry_ref)

    neg_c_sp = -LRU_C * jax.nn.softplus(-lam_ref[...])
    for bb in range(nb):
        xc = xc_ref[bb]
        xb = xc.astype(BF16)

        def block_diag(w_ref, b_ref):
            parts = [_dot(xb[:, n * LRU_BLOCK_W:(n + 1) * LRU_BLOCK_W], w_ref[n])
                     for n in range(LRU_BLOCKS)]
            return jnp.concatenate(parts, axis=-1) + b_ref[...]

        r = jax.nn.sigmoid(block_diag(wa_ref, ba_ref))
        ig = jax.nn.sigmoid(block_diag(wi_ref, bi_ref))
        log_a = neg_c_sp * r
        a = jnp.exp(log_a)
        u = jnp.sqrt(1.0 - a * a) * (ig * xc)
        for g in range(groups):
            a_bufs[bb][pl.ds(g * pitch, ts), :] = a[:, g * LANES:(g + 1) * LANES]
            u_bufs[bb][pl.ds(g * pitch, ts), :] = u[:, g * LANES:(g + 1) * LANES]

    def step(t, hs):
        tt = (ts - 1 - t) if reverse else t
        new = []
        for bb in range(nb):
            a_t = a_bufs[bb][pl.ds(tt, groups, stride=pitch), :]
            u_t = u_bufs[bb][pl.ds(tt, groups, stride=pitch), :]
            h = a_t * hs[bb] + u_t
            h_bufs[bb][pl.ds(tt, groups, stride=pitch), :] = h
            new.append(h)
        return tuple(new)

    hs = lax.fori_loop(0, ts, step, tuple(carry_ref[bb] for bb in range(nb)), unroll=8)
    for bb in range(nb):
        carry_ref[bb] = hs[bb]
        h_full = jnp.concatenate([h_bufs[bb][pl.ds(g * pitch, ts), :] for g in range(groups)], axis=-1)
        if final:
            out_ref[bb] = ((hf_ref[bb] + h_full) * gg_ref[bb]).astype(BF16)
        else:
            out_ref[bb] = h_full


def _lru_scan(xc, w_a, b_a, w_i, b_i, lam, hf=None, gg=None, *, ts, nb, reverse):
    B, S, W = xc.shape
    nc = S // ts
    final = hf is not None
    pitch = ts + SUBLANES
    cidx = (lambda b, i: (b, nc - 1 - i, 0)) if reverse else (lambda b, i: (b, i, 0))
    full = lambda shape: pl.BlockSpec(shape, lambda b, i: (0,) * len(shape))
    blk = pl.BlockSpec((nb, ts, W), cidx)
    in_specs = [blk, full((LRU_BLOCKS, LRU_BLOCK_W, LRU_BLOCK_W)), full((1, W)),
                full((LRU_BLOCKS, LRU_BLOCK_W, LRU_BLOCK_W)), full((1, W)), full((1, W))]
    args = [xc, w_a, b_a, w_i, b_i, lam]
    if final:
        in_specs += [blk, blk]
        args += [hf, gg]
    slab = pltpu.VMEM((W // LANES * pitch, LANES), F32)
    return pl.pallas_call(
        functools.partial(_lru_scan_body, ts=ts, nb=nb, reverse=reverse, final=final),
        grid=(B // nb, nc),
        in_specs=in_specs,
        out_specs=pl.BlockSpec((nb, ts, W), cidx),
        out_shape=jax.ShapeDtypeStruct((B, S, W), BF16 if final else F32),
        scratch_shapes=[slab] * (3 * nb) + [pltpu.VMEM((nb, W // LANES, LANES), F32)],
        compiler_params=_cparams(("parallel", "arbitrary")),
        name="lru_scan_bwd" if reverse else "lru_scan_fwd",
    )(*args)


def _post_mixer_body(x_ref, y_ref, w_ref, g_ref, wrh_ref, wrl_ref, br_ref,
                     x1_ref, xn_ref, mi_ref, mw_ref, cnt_ref, tri_ref):
    x1 = x_ref[...] + _dot(y_ref[...], w_ref[...])
    x1_ref[...] = x1
    xn = _rms(x1, g_ref[...])
    tm = xn.shape[0]
    _rows_to_tiles(xn_ref, xn)

    @pl.when(pl.program_id(0) == 0)
    def _():
        cnt_ref[...] = jnp.zeros_like(cnt_ref)
        r = lax.broadcasted_iota(I32, (tm, tm), 0)
        c = lax.broadcasted_iota(I32, (tm, tm), 1)
        tri_ref[...] = jnp.where(c < r, 1.0, 0.0).astype(BF16)

    xh = xn.astype(BF16)
    xl = (xn - xh.astype(F32)).astype(BF16)
    lg = _dot(xh, wrh_ref[...]) + (_dot(xl, wrh_ref[...]) + _dot(xh, wrl_ref[...])) + br_ref[...]
    lane = lax.broadcasted_iota(I32, (tm, LANES), 1)
    is_group = jnp.logical_and(lane >= N_EXPERTS, lane < N_EXPERTS + N_GROUPS)
    glog = jnp.where(is_group, lg, NEG_BIG)
    gmax = jnp.max(glog, axis=-1, keepdims=True)
    gsum = jnp.sum(jnp.where(is_group, jnp.exp(glog - gmax), 0.0), axis=-1, keepdims=True)
    g_top_p = 1.0 / gsum
    g_top = jnp.min(jnp.where(glog == gmax, lane, 4 * LANES), axis=-1, keepdims=True) - N_EXPERTS
    in_group = jnp.logical_and(lane < N_EXPERTS, (lane // EXPERTS_PER_GROUP) == g_top)
    sel = jnp.where(in_group, lg, NEG_BIG)
    v1 = jnp.max(sel, axis=-1, keepdims=True)
    i1 = jnp.min(jnp.where(sel == v1, lane, 4 * LANES), axis=-1, keepdims=True)
    sel2 = jnp.where(lane == i1, NEG_BIG, sel)
    v2 = jnp.max(sel2, axis=-1, keepdims=True)
    i2 = jnp.min(jnp.where(sel2 == v2, lane, 4 * LANES), axis=-1, keepdims=True)
    e2 = jnp.exp(v2 - v1)
    den = 1.0 + e2
    w1 = (1.0 / den) * g_top_p
    w2 = (e2 / den) * g_top_p
    hot1 = (lane == i1).astype(F32)
    hot2 = (lane == i2).astype(F32)
    hot = hot1 + hot2
    before = _dot(tri_ref[...], hot.astype(BF16)) + cnt_ref[...]
    r1 = jnp.sum(hot1 * before, axis=-1, keepdims=True).astype(I32)
    r2 = jnp.sum(hot2 * before, axis=-1, keepdims=True).astype(I32)
    cnt_ref[...] = cnt_ref[...] + jnp.sum(hot, axis=0, keepdims=True)
    mi_ref[...] = jnp.where(lane == 0, i1, jnp.where(lane == 1, i2,
                            jnp.where(lane == 2, r1, jnp.where(lane == 3, r2, 0))))
    mw_ref[...] = jnp.where(lane == 0, w1, jnp.where(lane == 1, w2, 0.0))


def _post_mixer(x, y, w, g, wr_hi, wr_lo, br, *, tm):
    T, D = x.shape
    K = y.shape[1]
    full = lambda shape: pl.BlockSpec(shape, lambda i: (0,) * len(shape))
    row = lambda n: pl.BlockSpec((tm, n), lambda i: (i, 0))
    return pl.pallas_call(
        _post_mixer_body,
        grid=(T // tm,),
        in_specs=[row(D), row(K), full((K, D)), full((1, D)),
                  full((D, LANES)), full((D, LANES)), full((1, LANES))],
        out_specs=[row(D), pl.BlockSpec((tm * SUBLANES, LANES), lambda i: (i, 0)),
                   row(LANES), row(LANES), full((1, LANES))],
        out_shape=[jax.ShapeDtypeStruct((T, D), F32), jax.ShapeDtypeStruct((T * SUBLANES, LANES), F32),
                   jax.ShapeDtypeStruct((T, LANES), I32), jax.ShapeDtypeStruct((T, LANES), F32),
                   jax.ShapeDtypeStruct((1, LANES), F32)],
        scratch_shapes=[pltpu.VMEM((tm, tm), BF16)],
        compiler_params=_cparams(("arbitrary",)),
        name="post_mixer",
    )(x, y, w, g, wr_hi, wr_lo, br)


def _route_tables(mi, cnt, *, te):
    T = mi.shape[0]
    nt = (2 * T) // te + N_EXPERTS
    counts = cnt[0, :N_EXPERTS].astype(I32)
    padded = ((counts + te - 1) // te) * te
    pend = jnp.cumsum(padded)
    pstart = pend - padded
    eids = jnp.arange(N_EXPERTS, dtype=I32)
    start_of = lambda e: jnp.sum(jnp.where(e[..., None] == eids, pstart, 0), axis=-1)
    pos = jnp.stack([start_of(mi[:, 0]) + mi[:, 2], start_of(mi[:, 1]) + mi[:, 3]], axis=-1).reshape(-1)
    n_used = pend[N_EXPERTS - 1] // te
    tile_start = jnp.minimum(jnp.arange(nt, dtype=I32), n_used - 1) * te
    tile_e = jnp.sum((tile_start[:, None] >= pend[None, :]).astype(I32), axis=-1)
    pad_start = pstart + counts
    n_valid = jnp.clip(jnp.sum(jnp.where(tile_e[:, None] == eids, pad_start, 0), axis=-1) - tile_start, 0, te)
    return (pos.astype(I32), tile_e.astype(I32), n_used.reshape(1).astype(I32), n_valid.astype(I32),
            pad_start.astype(I32), (padded - counts).astype(I32))


def _dispatch_body(pos_ref, nu_ref, ps_ref, pl_ref, xn_ref, xs_hbm, inv_ref, zbuf, sem, zsem, *, tm, te, nt):
    j = pl.program_id(0)
    base = j * (2 * tm)

    n_tok = pl.num_programs(0) * tm

    def body(r, c):
        for k in range(2):
            p = pos_ref[base + 2 * r + k]
            inv_ref[p] = k * n_tok + j * tm + r
            pltpu.make_async_copy(xn_ref.at[r], xs_hbm.at[p], sem).start(priority=k)
        return c

    lax.fori_loop(0, tm, body, 0, unroll=8)

    @pl.when(j == pl.num_programs(0) - 1)
    def _():
        zbuf[...] = jnp.zeros_like(zbuf)
        nu = nu_ref[0]

        def pad_copy(e):
            n = pl_ref[e]
            return pltpu.make_async_copy(zbuf.at[pl.ds(0, n)], xs_hbm.at[pl.ds(ps_ref[e], n)], zsem)

        def tail_copy(t):
            return pltpu.make_async_copy(zbuf, xs_hbm.at[pl.ds(t * te, te)], zsem)

        def pads(e, c, start):
            @pl.when(pl_ref[e] > 0)
            def _():
                pad_copy(e).start() if start else pad_copy(e).wait()
            return c

        def tails(t, c, start):
            tail_copy(t).start() if start else tail_copy(t).wait()
            return c

        lax.fori_loop(0, N_EXPERTS, functools.partial(pads, start=True), 0)
        lax.fori_loop(nu, nt, functools.partial(tails, start=True), 0)

        def clear(p, c):
            inv_ref[p] = 0
            return c

        def clear_pads(e, c):
            return lax.fori_loop(ps_ref[e], ps_ref[e] + pl_ref[e], clear, c)

        lax.fori_loop(0, N_EXPERTS, clear_pads, 0)
        lax.fori_loop(nu * te, nt * te, clear, 0)
        lax.fori_loop(0, N_EXPERTS, functools.partial(pads, start=False), 0)
        lax.fori_loop(nu, nt, functools.partial(tails, start=False), 0)

    for k in range(2):
        pltpu.make_async_copy(xn_ref, xs_hbm.at[pl.ds(0, tm)], sem).wait()


def _dispatch(xn3, pos, n_used, pad_start, pad_len, *, tm, te):
    T = xn3.shape[0]
    nt = (2 * T) // te + N_EXPERTS
    grid_spec = pltpu.PrefetchScalarGridSpec(
        num_scalar_prefetch=4,
        grid=(T // tm,),
        in_specs=[pl.BlockSpec((tm, SUBLANES, LANES), lambda j, *_: (j, 0, 0))],
        out_specs=[pl.BlockSpec(memory_space=pl.ANY), pl.BlockSpec(memory_space=pltpu.SMEM)],
        scratch_shapes=[pltpu.VMEM((te, SUBLANES, LANES), F32),
                        pltpu.SemaphoreType.DMA, pltpu.SemaphoreType.DMA],
    )
    return pl.pallas_call(
        functools.partial(_dispatch_body, tm=tm, te=te, nt=nt),
        grid_spec=grid_spec,
        out_shape=[jax.ShapeDtypeStruct((nt * te, SUBLANES, LANES), F32),
                   jax.ShapeDtypeStruct((nt * te,), I32)],
        compiler_params=_cparams(("arbitrary",)),
        name="moe_dispatch",
    )(pos, n_used, pad_start, pad_len, xn3)


def _experts_body(te_ref, nu_ref, nv_ref, inv_ref, xs_ref, wg_ref, wu_ref, wd_ref, out_hbm,
                  obuf, wgb, wub, wdb, sem, *, te):
    j = pl.program_id(0)
    nu = nu_ref[0]
    group = 8

    def tile_rows(i):
        return pl.ds(pl.multiple_of(i * SUBLANES, SUBLANES), SUBLANES)

    def scatter_start(tile):
        base = tile * te
        nv = nv_ref[tile]

        def copy_row(r, u):
            row = inv_ref[base + r]
            pltpu.make_async_copy(obuf.at[tile_rows(r), :], out_hbm.at[tile_rows(row), :],
                                  sem).start(priority=u % 2)

        def body(i, c):
            for u in range(group):
                copy_row(i * group + u, u)
            return c

        full = lax.shift_right_logical(nv, 3)
        lax.fori_loop(0, full, body, 0)
        for u in range(group):
            r = full * group + u

            @pl.when(r < nv)
            def _():
                copy_row(r, u)

    def scatter_wait(tile):
        n = pl.multiple_of(nv_ref[tile] * SUBLANES, SUBLANES)
        pltpu.make_async_copy(obuf.at[pl.ds(0, n), :], out_hbm.at[pl.ds(0, n), :], sem).wait()

    @pl.when(j < nu)
    def _():
        changed = jnp.logical_or(j == 0, te_ref[j] != te_ref[jnp.maximum(j - 1, 0)])

        @pl.when(changed)
        def _():
            wgb[...] = wg_ref[0, 0].astype(BF16)
            wub[...] = wu_ref[0, 0].astype(BF16)
            wdb[...] = wd_ref[0, 0].astype(BF16)

        x = _tiles_to_rows(xs_ref, te).astype(BF16)
        h = (jax.nn.silu(_dot(x, wgb[...])) * _dot(x, wub[...])).astype(BF16)
        res = _dot(h, wdb[...])

        @pl.when(j >= 1)
        def _():
            scatter_wait(j - 1)

        _rows_to_tiles(obuf, res)
        scatter_start(j)

        @pl.when(j == nu - 1)
        def _():
            scatter_wait(j)


def _experts(xs2, tile_e, n_used, n_valid, inv, w_gate, w_up, w_down, *, layer, te, T):
    nt = xs2.shape[0] // (te * SUBLANES)
    D = D_MODEL
    wspec = lambda shape: pl.BlockSpec((1, 1) + shape, lambda j, tile_e, *_: (layer, tile_e[j], 0, 0))
    grid_spec = pltpu.PrefetchScalarGridSpec(
        num_scalar_prefetch=4,
        grid=(nt,),
        in_specs=[pl.BlockSpec((te * SUBLANES, LANES), lambda j, tile_e, nu, *_: (jnp.minimum(j, nu[0] - 1), 0)),
                  wspec((D, D_EXPERT)), wspec((D, D_EXPERT)), wspec((D_EXPERT, D))],
        out_specs=pl.BlockSpec(memory_space=pl.ANY),
        scratch_shapes=[pltpu.VMEM((te * SUBLANES, LANES), F32),
                        pltpu.VMEM((D, D_EXPERT), BF16), pltpu.VMEM((D, D_EXPERT), BF16),
                        pltpu.VMEM((D_EXPERT, D), BF16), pltpu.SemaphoreType.DMA],
    )
    return pl.pallas_call(
        functools.partial(_experts_body, te=te),
        grid_spec=grid_spec,
        out_shape=jax.ShapeDtypeStruct((2 * T * SUBLANES, LANES), F32),
        compiler_params=_cparams(("arbitrary",)),
        name="moe_experts",
    )(tile_e, n_used, n_valid, inv, xs2, w_gate, w_up, w_down)


def _combine_body(x1_ref, mw_ref, o0_ref, o1_ref, y_ref):
    tm = x1_ref.shape[0]
    w = mw_ref[...]
    y_ref[...] = x1_ref[...] + (w[:, 0:1] * _tiles_to_rows(o0_ref, tm) + w[:, 1:2] * _tiles_to_rows(o1_ref, tm))


def _combine(x1, mw, out2, *, tm):
    T, D = x1.shape
    nb = T // tm
    return pl.pallas_call(
        _combine_body,
        grid=(nb,),
        in_specs=[pl.BlockSpec((tm, D), lambda j: (j, 0)),
                  pl.BlockSpec((tm, LANES), lambda j: (j, 0)),
                  pl.BlockSpec((tm * SUBLANES, LANES), lambda j: (j, 0)),
                  pl.BlockSpec((tm * SUBLANES, LANES), lambda j: (j + nb, 0))],
        out_specs=pl.BlockSpec((tm, D), lambda j: (j, 0)),
        out_shape=jax.ShapeDtypeStruct((T, D), F32),
        compiler_params=_cparams(("parallel",)),
        name="moe_combine",
    )(x1, mw, out2, out2)


def _mla_proj_body(x_ref, g_ref, win_ref, gq_ref, wuq_ref, gkv_ref, wuk_ref, wuv_ref,
                   gqh_ref, gkh_ref, cos_ref, sa_ref, sb_ref, q_ref, k_ref, v_ref):
    h = _rms(x_ref[0], g_ref[...]).astype(BF16)
    c = _dot(h, win_ref[...])
    cq = _rms(c[:, :Q_LORA], gq_ref[...]).astype(BF16)
    ckv = _rms(c[:, Q_LORA:Q_LORA + KV_LORA], gkv_ref[...]).astype(BF16)
    kpe = c[:, Q_LORA + KV_LORA:]
    q = _dot(cq, wuq_ref[...])
    kn = _dot(ckv, wuk_ref[...])
    v_ref[0] = _dot(ckv, wuv_ref[...]).astype(BF16)
    cos_t, sin_a, sin_b = cos_ref[...], sa_ref[...], sb_ref[...]
    gqh, gkh = gqh_ref[...], gkh_ref[...]

    def rope(r):
        return r * cos_t + pltpu.roll(r, LANES - QK_ROPE // 2, 1) * sin_a + pltpu.roll(r, QK_ROPE // 2, 1) * sin_b

    kpe_sq = kpe * kpe
    kpe_rot = rope(kpe * gkh[:, LANES:])
    for hh in range(N_HEADS):
        lo = hh * HEAD_PAD
        qn = q[:, lo:lo + LANES]
        qr = q[:, lo + LANES:lo + HEAD_PAD]
        ms = jnp.sum(qn * qn + qr * qr, axis=-1, keepdims=True) / QK_HEAD
        sc = lax.rsqrt(ms + EPS) * Q_SCALE
        q_ref[0, :, lo:lo + LANES] = (qn * sc * gqh[:, :LANES]).astype(BF16)
        q_ref[0, :, lo + LANES:lo + HEAD_PAD] = rope(qr * sc * gqh[:, LANES:]).astype(BF16)
        kh = kn[:, hh * QK_NOPE:(hh + 1) * QK_NOPE]
        ms = jnp.sum(kh * kh + kpe_sq, axis=-1, keepdims=True) / QK_HEAD
        sc = lax.rsqrt(ms + EPS)
        k_ref[0, :, lo:lo + LANES] = (kh * sc * gkh[:, :LANES]).astype(BF16)
        k_ref[0, :, lo + LANES:lo + HEAD_PAD] = (kpe_rot * sc).astype(BF16)


def _mla_proj(x, g, w_in, gq, w_uq, gkv, w_uk, w_uv, gqh, gkh, cos_t, sin_a, sin_b, *, tm):
    B, S, D = x.shape
    full = lambda shape: pl.BlockSpec(shape, lambda b, i: (0,) * len(shape))
    pos = pl.BlockSpec((tm, LANES), lambda b, i: (i, 0))
    HP = N_HEADS * HEAD_PAD
    HV = N_HEADS * V_HEAD
    return pl.pallas_call(
        _mla_proj_body,
        grid=(B, S // tm),
        in_specs=[pl.BlockSpec((1, tm, D), lambda b, i: (b, i, 0)), full((1, D)),
                  full(w_in.shape), full((1, Q_LORA)), full(w_uq.shape), full((1, KV_LORA)),
                  full(w_uk.shape), full(w_uv.shape), full((1, HEAD_PAD)), full((1, HEAD_PAD)),
                  pos, pos, pos],
        out_specs=[pl.BlockSpec((1, tm, HP), lambda b, i: (b, i, 0)),
                   pl.BlockSpec((1, tm, HP), lambda b, i: (b, i, 0)),
                   pl.BlockSpec((1, tm, HV), lambda b, i: (b, i, 0))],
        out_shape=[jax.ShapeDtypeStruct((B, S, HP), BF16), jax.ShapeDtypeStruct((B, S, HP), BF16),
                   jax.ShapeDtypeStruct((B, S, HV), BF16)],
        compiler_params=_cparams(("parallel", "parallel")),
        name="mla_proj",
    )(x, g, w_in, gq, w_uq, gkv, w_uk, w_uv, gqh, gkh, cos_t, sin_a, sin_b)


def _attn_body(q_ref, k_ref, v_ref, o_ref, *s_refs, tk, nkv):
    q = q_ref[0]
    tq = q.shape[0]

    def scores(c):
        kc = k_ref[0, pl.ds(pl.multiple_of(c * tk, tk), tk), :]
        return lax.dot_general(kc, q, (((1,), (1,)), ((), ())), preferred_element_type=F32)

    def update(s_ref, c, carry):
        m, l, acc = carry
        s = s_ref[...]
        m_new = jnp.maximum(m, jnp.max(s, axis=0, keepdims=True))
        alpha = jnp.exp2(m - m_new)
        p = jnp.exp2(s - m_new)
        l = alpha * l + jnp.sum(p, axis=0, keepdims=True)
        vc = v_ref[0, pl.ds(pl.multiple_of(c * tk, tk), tk), :]
        pv = lax.dot_general(vc, p.astype(BF16), (((0,), (0,)), ((), ())), preferred_element_type=F32)
        return m_new, l, alpha * acc + pv

    nbuf = len(s_refs)
    s_refs[0][...] = scores(0)

    def group(i, carry):
        c0 = nbuf * i
        for u in range(nbuf):
            s_refs[(u + 1) % nbuf][...] = scores(jnp.minimum(c0 + u + 1, nkv - 1))
            carry = update(s_refs[u], c0 + u, carry)
        return carry

    init = (jnp.full((1, tq), -jnp.inf, F32), jnp.zeros((1, tq), F32), jnp.zeros((V_HEAD, tq), F32))
    _, l, acc = lax.fori_loop(0, nkv // nbuf, group, init)
    o_ref[0] = jnp.transpose(acc / l).astype(BF16)


def _attention(q, k, v, *, tq, tk, nbuf):
    B, S, _ = q.shape
    nkv = S // tk
    assert nkv % nbuf == 0, (S, tk, nbuf)
    return pl.pallas_call(
        functools.partial(_attn_body, tk=tk, nkv=nkv),
        grid=(B, N_HEADS, S // tq),
        in_specs=[pl.BlockSpec((1, tq, HEAD_PAD), lambda b, h, i: (b, i, h)),
                  pl.BlockSpec((1, S, HEAD_PAD), lambda b, h, i: (b, 0, h)),
                  pl.BlockSpec((1, S, V_HEAD), lambda b, h, i: (b, 0, h))],
        out_specs=pl.BlockSpec((1, tq, V_HEAD), lambda b, h, i: (b, i, h)),
        out_shape=jax.ShapeDtypeStruct((B, S, N_HEADS * V_HEAD), BF16),
        scratch_shapes=[pltpu.VMEM((tk, tq), F32)] * nbuf,
        compiler_params=_cparams(("parallel", "parallel", "arbitrary")),
        name="mla_attention",
    )(q, k, v)


def _rope_tables(S):
    inv = 1.0 / (ROPE_THETA ** (jnp.arange(0, QK_ROPE, 2, dtype=F32) / QK_ROPE))
    ang = jnp.arange(S, dtype=F32)[:, None] * inv[None, :]
    cos, sin = jnp.cos(ang), jnp.sin(ang)
    z = jnp.zeros_like(cos)
    cos_t = jnp.concatenate([cos, cos, z, z], axis=-1)
    sin_a = jnp.concatenate([-sin, z, z, z], axis=-1)
    sin_b = jnp.concatenate([z, sin, z, z], axis=-1)
    return cos_t, sin_a, sin_b


def _head_pad_cols(w_nope, w_rope):
    K = w_nope.shape[0]
    z = jnp.zeros((K, N_HEADS, HEAD_PAD - QK_HEAD), w_nope.dtype)
    return jnp.concatenate([w_nope, w_rope, z], axis=-1).reshape(K, N_HEADS * HEAD_PAD)


def _router_params(w_group, b_group, w_router, b_router):
    pad = LANES - N_EXPERTS - N_GROUPS
    w = jnp.concatenate([w_router, w_group, jnp.zeros((D_MODEL, pad), F32)], axis=-1)
    b = jnp.concatenate([b_router, b_group, jnp.zeros((pad,), F32)]).reshape(1, LANES)
    w_hi = w.astype(BF16)
    w_lo = (w - w_hi.astype(F32)).astype(BF16)
    return w_hi, w_lo, b


def _tile(n, pref):
    t = min(n, pref)
    assert n % t == 0, (n, t)
    return t


def _moe(x, y, w_proj, g, router, w_gate, w_up, w_down, *, layer, final_shape):
    T = x.shape[0]
    tm = _tile(T, 512)
    te = _tile(T, 256)
    x1, xn2, mi, mw, cnt = _post_mixer(x, y, w_proj, g, *router, tm=tm)
    pos, tile_e, n_used, n_valid, pad_start, pad_len = _route_tables(mi, cnt, te=te)
    xs3, inv = _dispatch(xn2.reshape(T, SUBLANES, LANES), pos, n_used, pad_start, pad_len, tm=te, te=te)
    out2 = _experts(xs3.reshape(-1, LANES), tile_e, n_used, n_valid, inv, w_gate, w_up, w_down,
                    layer=layer, te=te, T=T)
    return _combine(x1, mw, out2, tm=tm).reshape(final_shape)


def _trunk(x, p):
    B, S, D = x.shape
    T = B * S
    ts = _tile(S, 256)
    nb = 2 if B % 2 == 0 else 1
    gg, xc = _lru_in(x, p["mix_g"][0], p["lru_w_gate"], p["lru_w_rec"], p["conv_w"], p["conv_b"], ts=ts)
    hf = _lru_scan(xc, p["w_a"][0], p["b_a"][0], p["w_i"][0], p["b_i"][0], p["lam"][0],
                   ts=ts, nb=nb, reverse=False)
    y = _lru_scan(xc, p["w_a"][1], p["b_a"][1], p["w_i"][1], p["b_i"][1], p["lam"][1], hf, gg,
                  ts=ts, nb=nb, reverse=True)
    x = _moe(x.reshape(T, D), y.reshape(T, LRU_WIDTH), p["lru_w_out"], p["ffn_g"][0], p["router"][0],
             p["moe_w_gate"], p["moe_w_up"], p["moe_w_down"], layer=0, final_shape=(B, S, D))
    tm = _tile(S, 256)
    cos_t, sin_a, sin_b = _rope_tables(S)
    q, k, v = _mla_proj(x, p["mix_g"][1], p["mla_w_in"], p["gq"], p["w_uq"], p["gkv"], p["w_uk"], p["w_uv"],
                        p["gqh"], p["gkh"], cos_t, sin_a, sin_b, tm=tm)
    tk = _tile(S, 512)
    o = _attention(q, k, v, tq=_tile(S, 512), tk=tk, nbuf=min(4, S // tk))
    x = _moe(x.reshape(T, D), o.reshape(T, N_HEADS * V_HEAD), p["mla_w_o"], p["ffn_g"][1], p["router"][1],
             p["moe_w_gate"], p["moe_w_up"], p["moe_w_down"], layer=1, final_shape=(B, S, D))
    return x


def _prepare(mix_norm, ffn_norm, lru_w_in, lru_conv_w, lru_conv_b, lru_w_a, lru_b_a, lru_w_i, lru_b_i,
             lru_lambda, lru_w_out, mla_w_in, mla_q_lat_norm, mla_w_uq, mla_kv_lat_norm, mla_w_ukv,
             mla_q_head_norm, mla_k_head_norm, mla_w_o, moe_w_group, moe_b_group, moe_w_router,
             moe_b_router, moe_w_gate, moe_w_up, moe_w_down):
    W = LRU_WIDTH
    w_uq = mla_w_uq[0].reshape(Q_LORA, N_HEADS, QK_HEAD)
    w_ukv = mla_w_ukv[0].reshape(KV_LORA, N_HEADS, QK_NOPE + V_HEAD)
    head_gain = lambda g: jnp.concatenate([g, jnp.zeros((HEAD_PAD - QK_HEAD,), F32)]).reshape(1, HEAD_PAD)
    w_in = jnp.concatenate([mla_w_in[0], jnp.zeros((D_MODEL, LANES - QK_ROPE), F32)], axis=-1)
    return {
        "mix_g": mix_norm.reshape(-1, 1, D_MODEL),
        "ffn_g": ffn_norm.reshape(-1, 1, D_MODEL),
        "lru_w_gate": lru_w_in[0, :, :W].astype(BF16),
        "lru_w_rec": lru_w_in[0, :, W:].astype(BF16),
        "conv_w": lru_conv_w[0],
        "conv_b": lru_conv_b[0].reshape(1, W),
        "w_a": lru_w_a[0].astype(BF16),
        "b_a": lru_b_a[0].reshape(2, 1, W),
        "w_i": lru_w_i[0].astype(BF16),
        "b_i": lru_b_i[0].reshape(2, 1, W),
        "lam": lru_lambda[0].reshape(2, 1, W),
        "lru_w_out": lru_w_out[0].astype(BF16),
        "mla_w_in": w_in.astype(BF16),
        "gq": mla_q_lat_norm[0].reshape(1, Q_LORA),
        "w_uq": _head_pad_cols(w_uq[:, :, :QK_NOPE], w_uq[:, :, QK_NOPE:]).astype(BF16),
        "gkv": mla_kv_lat_norm[0].reshape(1, KV_LORA),
        "w_uk": w_ukv[:, :, :QK_NOPE].reshape(KV_LORA, N_HEADS * QK_NOPE).astype(BF16),
        "w_uv": w_ukv[:, :, QK_NOPE:].reshape(KV_LORA, N_HEADS * V_HEAD).astype(BF16),
        "gqh": head_gain(mla_q_head_norm[0]),
        "gkh": head_gain(mla_k_head_norm[0]),
        "mla_w_o": mla_w_o[0].astype(BF16),
        "router": [_router_params(moe_w_group[l], moe_b_group[l], moe_w_router[l], moe_b_router[l])
                   for l in range(2)],
        "moe_w_gate": moe_w_gate, "moe_w_up": moe_w_up, "moe_w_down": moe_w_down,
    }


def kernel(x_prompt, x_sample, mix_norm, ffn_norm, lru_w_in, lru_conv_w, lru_conv_b, lru_w_a, lru_b_a, lru_w_i, lru_b_i, lru_lambda, lru_w_out, mla_w_in, mla_q_lat_norm, mla_w_uq, mla_kv_lat_norm, mla_w_ukv, mla_q_head_norm, mla_k_head_norm, mla_w_o, moe_w_group, moe_b_group, moe_w_router, moe_b_router, moe_w_gate, moe_w_up, moe_w_down):
    p = _prepare(mix_norm, ffn_norm, lru_w_in, lru_conv_w, lru_conv_b, lru_w_a, lru_b_a, lru_w_i, lru_b_i,
                 lru_lambda, lru_w_out, mla_w_in, mla_q_lat_norm, mla_w_uq, mla_kv_lat_norm, mla_w_ukv,
                 mla_q_head_norm, mla_k_head_norm, mla_w_o, moe_w_group, moe_b_group, moe_w_router,
                 moe_b_router, moe_w_gate, moe_w_up, moe_w_down)
    return (_trunk(x_prompt, p), _trunk(x_sample, p))
```
